```python
import math
import jax
import jax.numpy as jnp
from jax import lax
import numpy as np


D_MODEL = 1024
BATCH = 8
SEQ = 4096
DEPTH = 1

GRID_W = 64
CTX_LEN = 256
RET_HEADS = 4
RET_DK = D_MODEL // RET_HEADS
RET_DV = 2 * RET_DK
RET_CHUNK = 128
DIFF_DH = 64
DIFF_HEADS = D_MODEL // (2 * DIFF_DH)
DIFF_BLOCK = 128
D_FF = ((8 * D_MODEL // 3 + 127) // 128) * 128
CONV_W = 3
ROPE_BASE = 10000.0
LN_EPS = 1e-5
RET_Q = RET_HEADS * RET_DK
RET_V = RET_HEADS * RET_DV
DIFF_QK = DIFF_HEADS * 2 * DIFF_DH
DIFF_V = DIFF_HEADS * 2 * DIFF_DH
IN_SIZES = (RET_Q, RET_Q, RET_V, RET_V, DIFF_QK, DIFF_QK, DIFF_V, D_MODEL, D_MODEL)
N_IN = 2 * RET_Q + 2 * RET_V + 2 * DIFF_QK + DIFF_V + 2 * D_MODEL

kernel_name = 'hybrid_retention_diffattn_dit_layer'


def layer_norm(x, g, b):
    xf = x.astype(jnp.float32)
    mu = xf.mean(-1, keepdims=True)
    var = jnp.mean(jnp.square(xf - mu), -1, keepdims=True)
    return ((xf - mu) * lax.rsqrt(var + LN_EPS) * g + b).astype(x.dtype)


def axial_rope(row, col, head_dim):
    half = head_dim // 2
    inv = ROPE_BASE ** (-(jnp.arange(0, half, 2, dtype=jnp.float32) / half))
    ang = jnp.concatenate([row[:, None] * inv, col[:, None] * inv], axis=-1)
    return jnp.cos(ang), jnp.sin(ang)


def apply_rope(x, cos, sin):
    d = x.shape[-1] // 2
    x1, x2 = x[..., :d], x[..., d:]
    return jnp.concatenate([x1 * cos - x2 * sin, x2 * cos + x1 * sin], axis=-1)


def split_in(p):
    offs = np.cumsum(IN_SIZES)[:-1].tolist()
    return jnp.split(p, offs, axis=-1)


def to_heads(t, n):
    B, T, _ = t.shape
    return t.reshape(B, T, n, -1).transpose(0, 2, 1, 3)


def merge_heads(t):
    B, H, T, d = t.shape
    return t.transpose(0, 2, 1, 3).reshape(B, T, H * d)


def diff_qk_heads(t):
    B, T, _ = t.shape
    return t.reshape(B, T, DIFF_HEADS, 2, DIFF_DH).transpose(0, 2, 3, 1, 4)


def project_heads(p, rope_ret=None, rope_dif=None):
    qr, kr, vr, gr, qd, kd, vd, gate_r, gate_d = split_in(p)
    qr, kr, vr = to_heads(qr, RET_HEADS), to_heads(kr, RET_HEADS), to_heads(vr, RET_HEADS)
    qd, kd = diff_qk_heads(qd), diff_qk_heads(kd)
    vd = to_heads(vd, DIFF_HEADS)
    if rope_ret is not None:
        qr, kr = apply_rope(qr, *rope_ret), apply_rope(kr, *rope_ret)
        qd, kd = apply_rope(qd, *rope_dif), apply_rope(kd, *rope_dif)
    return (qr, kr * RET_DK ** -0.5, vr, gr, qd * DIFF_DH ** -0.5, kd, vd, gate_r, gate_d)


def retention_chunked(q, k, v, log_g, s0):
    B, H, T, _ = q.shape
    dv = v.shape[-1]
    n = T // RET_CHUNK
    pos = jnp.arange(RET_CHUNK, dtype=jnp.float32)
    rel = pos[:, None] - pos[None, :]
    d_in = jnp.where(rel >= 0, jnp.exp(jnp.maximum(rel, 0.0) * log_g[:, None, None]), 0.0)
    d_q = jnp.exp((pos + 1.0) * log_g[:, None])[..., None]
    d_k = jnp.exp((RET_CHUNK - 1.0 - pos) * log_g[:, None])[..., None]
    d_s = jnp.exp(RET_CHUNK * log_g)[:, None, None]

    def chunks(t):
        return t.reshape(B, H, n, RET_CHUNK, t.shape[-1]).transpose(2, 0, 1, 3, 4)

    def step(s, qkv):
        qc, kc, vc = qkv
        inner = jnp.einsum('bhqd,bhkd->bhqk', qc, kc) * d_in
        y = jnp.einsum('bhqk,bhkv->bhqv', inner, vc) + jnp.einsum('bhqd,bhdv->bhqv', qc, s) * d_q
        s = s * d_s + jnp.einsum('bhkd,bhkv->bhdv', kc * d_k, vc)
        return s, y

    s_fin, ys = lax.scan(step, s0, (chunks(q), chunks(k), chunks(v)))
    return ys.transpose(1, 2, 0, 3, 4).reshape(B, H, T, dv), s_fin


def context_state(k, v, log_g, reverse):
    P = k.shape[2]
    pos = jnp.arange(P, dtype=jnp.float32)
    dist = pos if reverse else (P - 1.0) - pos
    w = jnp.exp(dist[None, :] * log_g[:, None])
    return jnp.einsum('bhtd,bhtv->bhdv', k * w[None, :, :, None], v)


def flip_t(t):
    return t[:, :, ::-1]


def diff_attention(q, k, v, lam):
    B, H, _, T, d = q.shape
    nb = T // DIFF_BLOCK
    qb = q.reshape(B, H, 2, nb, DIFF_BLOCK, d).transpose(3, 0, 1, 2, 4, 5)

    def one(qblk):
        s = jnp.einsum('bhcqd,bhckd->bhcqk', qblk, k)
        p = jax.nn.softmax(s, axis=-1)
        a = p[:, :, 0] - lam * p[:, :, 1]
        return jnp.einsum('bhqk,bhkv->bhqv', a, v)

    o = lax.map(one, qb)
    return o.transpose(1, 2, 0, 3, 4).reshape(B, H, T, v.shape[-1])


def retention_readout(y, g, w):
    mu = y.mean(-1, keepdims=True)
    var = jnp.mean(jnp.square(y - mu), -1, keepdims=True)
    y = merge_heads((y - mu) * lax.rsqrt(var + LN_EPS))
    return (jax.nn.silu(g) * y) @ w


def diff_readout(o, g_sub, lam_init, w):
    o = o * lax.rsqrt(jnp.mean(jnp.square(o), -1, keepdims=True) + LN_EPS) * g_sub * (1.0 - lam_init)
    return merge_heads(o) @ w


def merge_branches(y_ret, y_dif, gate_r, gate_d, b_gate, w_o):
    gr = jax.nn.sigmoid(gate_r + b_gate[:D_MODEL])
    gd = jax.nn.sigmoid(gate_d + b_gate[D_MODEL:])
    return (gr * y_ret + gd * y_dif) @ w_o


def dwconv_centred(a, w, b):
    pad = CONV_W // 2
    T = a.shape[1]
    ap = jnp.pad(a, ((0, 0), (pad, pad), (0, 0)))
    out = b
    for j in range(CONV_W):
        out = out + ap[:, j:j + T] * w[j]
    return out


def conv_ffn(h, w_up, conv_w, conv_b, w_down):
    u, gate = jnp.split(h @ w_up, 2, axis=-1)
    u = dwconv_centred(u, conv_w, conv_b)
    return (jax.nn.gelu(u, approximate=False) * gate) @ w_down


def setup_inputs(seed: int = 0) -> dict:
    key = jax.random.key(seed)
    ks = jax.random.split(key, 24)
    L, D = DEPTH, D_MODEL
    beta = (8.0 * DEPTH) ** -0.25

    def nrm(k, shape, scale):
        return scale * jax.random.normal(k, shape, jnp.float32)

    gamma0 = 1.0 - 2.0 ** (-5.0 - np.arange(RET_HEADS, dtype=np.float32))
    logit0 = jnp.asarray(np.log(gamma0 / (1.0 - gamma0)), jnp.float32)
    return {
        'x': nrm(ks[0], (BATCH, SEQ, D), 1.0),
        'c': nrm(ks[1], (BATCH, D), 1.0),
        'ctx': nrm(ks[2], (BATCH, CTX_LEN, D), 1.0),
        'c_ctx': nrm(ks[3], (D,), 1.0),
        'ln_in_g': 1.0 + nrm(ks[4], (D,), 0.02),
        'ln_in_b': nrm(ks[5], (D,), 0.02),
        'w_mod': nrm(ks[6], (L, D, 6 * D), 0.5 * D ** -0.5),
        'b_mod': nrm(ks[7], (L, 6 * D), 0.02),
        'w_in': nrm(ks[8], (L, D, N_IN), D ** -0.5),
        'b_gate': nrm(ks[9], (L, 2 * D), 0.02),
        'ret_decay_logit': logit0 + nrm(ks[10], (L, 2, RET_HEADS), 0.1),
        'diff_lambda': nrm(ks[11], (L, 4, DIFF_DH), 0.1),
        'diff_subln_g': 1.0 + nrm(ks[12], (L, 2 * DIFF_DH), 0.02),
        'w_ret_out': nrm(ks[13], (L, RET_V, D), RET_V ** -0.5),
        'w_diff_out': nrm(ks[14], (L, DIFF_V, D), DIFF_V ** -0.5),
        'w_o': nrm(ks[15], (L, D, D), beta * D ** -0.5),
        'ln1_g': 1.0 + nrm(ks[16], (L, D), 0.02),
        'ln1_b': nrm(ks[17], (L, D), 0.02),
        'w_up': nrm(ks[18], (L, D, 2 * D_FF), D ** -0.5),
        'conv_w': nrm(ks[19], (L, CONV_W, D_FF), CONV_W ** -0.5),
        'conv_b': nrm(ks[20], (L, D_FF), 0.02),
        'w_down': nrm(ks[21], (L, D_FF, D), beta * D_FF ** -0.5),
        'ln2_g': 1.0 + nrm(ks[22], (L, D), 0.02),
        'ln2_b': nrm(ks[23], (L, D), 0.02),
    }


def reference(x, c, ctx, c_ctx, ln_in_g, ln_in_b, w_mod, b_mod, w_in, b_gate, ret_decay_logit,
              diff_lambda, diff_subln_g, w_ret_out, w_diff_out, w_o, ln1_g, ln1_b,
              w_up, conv_w, conv_b, w_down, ln2_g, ln2_b):
    f32 = jnp.float32
    dtype = x.dtype
    S = x.shape[1]
    ROWS = S // GRID_W
    row = jnp.repeat(jnp.arange(ROWS, dtype=f32), GRID_W)
    col = jnp.tile(jnp.arange(GRID_W, dtype=f32), ROWS)
    rope_ret = axial_rope(row, col, RET_DK)
    rope_dif = axial_rope(row, col, DIFF_DH)
    alpha = (2.0 * DEPTH) ** 0.25

    x = layer_norm(x, ln_in_g, ln_in_b)
    xc = layer_norm(ctx, ln_in_g, ln_in_b)
    cond = jax.nn.silu(c)
    cond_ctx = jax.nn.silu(c_ctx)

    for i in range(DEPTH):
        ctx_out = i < DEPTH - 1
        lam_init = 0.8 - 0.6 * math.exp(-0.3 * i)
        mod = cond @ w_mod[i] + b_mod[i]
        mod_c = cond_ctx @ w_mod[i] + b_mod[i]
        sh1, sc1, g1, sh2, sc2, g2 = jnp.split(mod[:, None, :], 6, axis=-1)
        sh1c, sc1c, g1c, sh2c, sc2c, g2c = jnp.split(mod_c, 6)

        lq1, lk1, lq2, lk2 = diff_lambda[i].astype(f32)
        lam = jnp.exp(jnp.sum(lq1 * lk1)) - jnp.exp(jnp.sum(lq2 * lk2)) + lam_init
        log_g = jax.nn.log_sigmoid(ret_decay_logit[i].astype(f32))

        p = ((x * (1 + sc1) + sh1) @ w_in[i]).astype(f32)
        pc = ((xc * (1 + sc1c) + sh1c) @ w_in[i]).astype(f32)
        qr, kr, vr, gr, qd, kd, vd, gate_r, gate_d = project_heads(p, rope_ret, rope_dif)
        qr_c, kr_c, vr_c, gr_c, qd_c, kd_c, vd_c, gate_r_c, gate_d_c = project_heads(pc)

        if ctx_out:
            zeros = jnp.zeros(kr_c.shape[:2] + (RET_DK, RET_DV), f32)
            yc_f, s_f = retention_chunked(qr_c, kr_c, vr_c, log_g[0], zeros)
            yc_b, s_b = retention_chunked(flip_t(qr_c), flip_t(kr_c), flip_t(vr_c), log_g[1], zeros)
            yc_ret = yc_f + flip_t(yc_b)
        else:
            s_f = context_state(kr_c, vr_c, log_g[0], reverse=False)
            s_b = context_state(kr_c, vr_c, log_g[1], reverse=True)
        y_f, _ = retention_chunked(qr, kr, vr, log_g[0], s_f)
        y_b, _ = retention_chunked(flip_t(qr), flip_t(kr), flip_t(vr), log_g[1], s_b)
        y_ret = retention_readout(y_f + flip_t(y_b), gr, w_ret_out[i])

        k_all = jnp.concatenate([kd, kd_c], axis=3)
        v_all = jnp.concatenate([vd, vd_c], axis=2)
        y_dif = diff_readout(diff_attention(qd, k_all, v_all, lam), diff_subln_g[i], lam_init, w_diff_out[i])

        y_mix = merge_branches(y_ret, y_dif, gate_r, gate_d, b_gate[i], w_o[i]).astype(dtype)
        x = layer_norm(alpha * x + g1 * y_mix, ln1_g[i], ln1_b[i])

        if ctx_out:
            yc_r = retention_readout(yc_ret, gr_c, w_ret_out[i])
            yc_d = diff_readout(diff_attention(qd_c, kd_c, vd_c, lam), diff_subln_g[i], lam_init, w_diff_out[i])
            yc_mix = merge_branches(yc_r, yc_d, gate_r_c, gate_d_c, b_gate[i], w_o[i]).astype(dtype)
            xc = layer_norm(alpha * xc + g1c * yc_mix, ln1_g[i], ln1_b[i])

        y_ff = conv_ffn(x * (1 + sc2) + sh2, w_up[i], conv_w[i], conv_b[i], w_down[i])
        x = layer_norm(alpha * x + g2 * y_ff, ln2_g[i], ln2_b[i])
        if ctx_out:
            yc_ff = conv_ffn(xc * (1 + sc2c) + sh2c, w_up[i], conv_w[i], conv_b[i], w_down[i])
            xc = layer_norm(alpha * xc + g2c * yc_ff, ln2_g[i], ln2_b[i])

    return x
```

```python
import functools
import math

import numpy as np
import jax
import jax.numpy as jnp
from jax import lax
from jax.experimental import pallas as pl
from jax.experimental.pallas import tpu as pltpu

F32 = jnp.float32
BF16 = jnp.bfloat16

D_MODEL = 1024
GRID_W = 64
RET_HEADS = 4
RET_DK = D_MODEL // RET_HEADS
RET_DV = 2 * RET_DK
DIFF_DH = 64
DIFF_HEADS = D_MODEL // (2 * DIFF_DH)
D_FF = ((8 * D_MODEL // 3 + 127) // 128) * 128
CONV_W = 3
ROPE_BASE = 10000.0
LN_EPS = 1e-5
DEPTH = 1
ALPHA = (2.0 * DEPTH) ** 0.25
LAM_INIT = 0.8 - 0.6 * math.exp(-0.3 * 0)

OFF_QR, OFF_KR, OFF_VR, OFF_GR = 0, RET_HEADS * RET_DK, 2 * RET_HEADS * RET_DK, 2 * RET_HEADS * RET_DK + RET_HEADS * RET_DV
OFF_QD = OFF_GR + RET_HEADS * RET_DV
OFF_KD = OFF_QD + D_MODEL
OFF_VD = OFF_KD + D_MODEL
OFF_GATE_R = OFF_VD + D_MODEL
OFF_GATE_D = OFF_GATE_R + D_MODEL
N_IN = OFF_GATE_D + D_MODEL

LANES = 128
RET_CHUNK = 256
VMEM_LIMIT = 60 * 1024 * 1024


def _cparams(sem):
    return pltpu.CompilerParams(dimension_semantics=sem, vmem_limit_bytes=VMEM_LIMIT)


def _layer_norm(x, g, b):
    mu = jnp.mean(x, axis=-1, keepdims=True)
    xc = x - mu
    var = jnp.mean(xc * xc, axis=-1, keepdims=True)
    return xc * lax.rsqrt(var + LN_EPS) * g + b


def _dot(a, b):
    return jnp.dot(a, b, preferred_element_type=F32)


def _dot_nt(a, b):
    return lax.dot_general(a, b, (((1,), (1,)), ((), ())), preferred_element_type=F32)


def _dot_tn(a, b):
    return lax.dot_general(a, b, (((0,), (0,)), ((), ())), preferred_element_type=F32)


def _mod_kernel(c_ref, w_ref, b_ref, o_ref):
    cond = jax.nn.silu(c_ref[...])
    o_ref[...] = _dot(cond.astype(BF16), w_ref[...].astype(BF16)) + b_ref[...]


def _mod_call(cc, w_mod, b_mod):
    rows, d = cc.shape
    n = w_mod.shape[1]
    tn = 1024
    return pl.pallas_call(
        _mod_kernel,
        out_shape=jax.ShapeDtypeStruct((rows, n), F32),
        grid=(n // tn,),
        in_specs=[pl.BlockSpec((rows, d), lambda j: (0, 0)),
                  pl.BlockSpec((d, tn), lambda j: (0, j)),
                  pl.BlockSpec((1, tn), lambda j: (0, j))],
        out_specs=pl.BlockSpec((rows, tn), lambda j: (0, j)),
        compiler_params=_cparams(("arbitrary",)),
        name="mod",
    )(cc, w_mod, b_mod)


def _ln_mod_kernel(x_ref, g_ref, b_ref, mod_ref, *out_refs, emit_xn):
    xn = _layer_norm(x_ref[0], g_ref[...], b_ref[...])
    m = mod_ref[0]
    h = (xn * (1.0 + m[1:2]) + m[0:1]).astype(BF16)
    if emit_xn:
        out_refs[0][0] = xn
        out_refs[1][0] = h
    else:
        out_refs[0][0] = h


def _ln_mod_call(x, g, b, mod, *, mod_row, emit_xn, ts):
    bsz, s, d = x.shape
    ts = min(ts, s)
    mod_map = (lambda i, j: (i, 0, 0)) if mod_row is None else (lambda i, j: (mod_row, 0, 0))
    out_shape = [jax.ShapeDtypeStruct((bsz, s, d), BF16)]
    out_specs = [pl.BlockSpec((1, ts, d), lambda i, j: (i, j, 0))]
    if emit_xn:
        out_shape = [jax.ShapeDtypeStruct((bsz, s, d), F32)] + out_shape
        out_specs = [pl.BlockSpec((1, ts, d), lambda i, j: (i, j, 0))] + out_specs
    return pl.pallas_call(
        functools.partial(_ln_mod_kernel, emit_xn=emit_xn),
        out_shape=out_shape,
        grid=(bsz, s // ts),
        in_specs=[pl.BlockSpec((1, ts, d), lambda i, j: (i, j, 0)),
                  pl.BlockSpec((1, d), lambda i, j: (0, 0)),
                  pl.BlockSpec((1, d), lambda i, j: (0, 0)),
                  pl.BlockSpec((1, 6, d), mod_map)],
        out_specs=out_specs,
        compiler_params=_cparams(("arbitrary", "arbitrary")),
        name="ln_mod_x" if emit_xn else "ln_mod_ctx",
    )(x, g, b, mod)


def _proj_kernel(x_ref, w_ref, *rest, epi, scale, tn):
    acc = _dot(x_ref[...], w_ref[...])
    if epi == "plain":
        (o_ref,) = rest
        if scale != 1.0:
            acc = acc * scale
        o_ref[...] = acc.astype(o_ref.dtype)
    elif epi == "rope_ret":
        cos_ref, sin_ref, o_ref = rest
        cos, sin = cos_ref[0], sin_ref[0]
        for h in range(tn // (2 * LANES)):
            a = h * 2 * LANES
            x1 = acc[:, a:a + LANES]
            x2 = acc[:, a + LANES:a + 2 * LANES]
            o_ref[:, a:a + LANES] = (x1 * cos - x2 * sin).astype(o_ref.dtype)
            o_ref[:, a + LANES:a + 2 * LANES] = (x2 * cos + x1 * sin).astype(o_ref.dtype)
    else:
        cos_ref, sin_ref, o_ref = rest
        cos, sin = cos_ref[0], sin_ref[0]
        for g in range(tn // LANES):
            a = g * LANES
            xg = acc[:, a:a + LANES]
            o_ref[:, a:a + LANES] = (xg * cos + pltpu.roll(xg, LANES // 2, axis=1) * sin).astype(o_ref.dtype)


def _proj_call(x, w, *, col0, ncols, out_dtype, epi="plain", scale=1.0, tables=None, seq=None, tm=1024, tn=1024, name="proj"):
    m, k = x.shape
    tm = min(tm, m if seq is None else seq)
    tn = min(tn, ncols)
    assert m % tm == 0 and ncols % tn == 0 and col0 % tn == 0
    jb = col0 // tn
    in_specs = [pl.BlockSpec((tm, k), lambda j, i: (i, 0)),
                pl.BlockSpec((k, tn), lambda j, i: (0, jb + j))]
    args = [x, w]
    if epi != "plain":
        assert seq % tm == 0
        ns = seq // tm
        tspec = pl.BlockSpec((1, tm, LANES), lambda j, i: (j, i % ns, 0))
        in_specs += [tspec, tspec]
        args += list(tables)
    return pl.pallas_call(
        functools.partial(_proj_kernel, epi=epi, scale=scale, tn=tn),
        out_shape=jax.ShapeDtypeStruct((m, ncols), out_dtype),
        grid=(ncols // tn, m // tm),
        in_specs=in_specs,
        out_specs=pl.BlockSpec((tm, tn), lambda j, i: (i, j)),
        compiler_params=_cparams(("arbitrary", "arbitrary")),
        name=name,
    )(*args)


def _ret_kernel(lg_ref, q_ref, k_ref, v_ref, g_ref, kc_ref, vc_ref, o_ref, y_scr, sf_scr, sb_scr, *, chunk, n_chunks):
    c_len = chunk
    lg = lg_ref[0]
    lgf = lg[0:1, :]
    lgb = lg[1:2, :]

    def col(x):
        return x[:, 0:1]

    p_len = kc_ref.shape[1]
    pos_p = lax.broadcasted_iota(jnp.int32, (p_len, LANES), 0).astype(F32)
    kc = kc_ref[0]
    vc = vc_ref[0]
    sf_scr[...] = _dot_tn((kc * col(jnp.exp(((p_len - 1.0) - pos_p) * lgf))).astype(BF16), vc)
    sb_scr[...] = _dot_tn((kc * col(jnp.exp(pos_p * lgb))).astype(BF16), vc)

    pos = lax.broadcasted_iota(jnp.int32, (c_len, LANES), 0).astype(F32)
    dq_f = col(jnp.exp((pos + 1.0) * lgf))
    dk_f = col(jnp.exp((c_len - 1.0 - pos) * lgf))
    dq_b = col(jnp.exp((c_len - pos) * lgb))
    dk_b = col(jnp.exp(pos * lgb))
    ds_f = col(jnp.exp(c_len * lgf))
    ds_b = col(jnp.exp(c_len * lgb))
    rel = (lax.broadcasted_iota(jnp.int32, (c_len, c_len), 0)
           - lax.broadcasted_iota(jnp.int32, (c_len, c_len), 1)).astype(F32)
    d_in = (jnp.where(rel >= 0, jnp.exp(jnp.maximum(rel, 0.0) * col(lgf)), 0.0)
            + jnp.where(rel <= 0, jnp.exp(jnp.maximum(-rel, 0.0) * col(lgb)), 0.0))

    def fwd(c, carry):
        off = pl.multiple_of(c * c_len, c_len)
        q = q_ref[0, pl.ds(off, c_len), :]
        k = k_ref[0, pl.ds(off, c_len), :]
        v = v_ref[0, pl.ds(off, c_len), :]
        inner = _dot_nt(q, k) * d_in
        s = sf_scr[...]
        y_scr[pl.ds(off, c_len), :] = _dot(inner.astype(BF16), v) + _dot(q, s.astype(BF16)) * dq_f
        sf_scr[...] = s * ds_f + _dot_tn((k.astype(F32) * dk_f).astype(BF16), v)
        return carry

    lax.fori_loop(0, n_chunks, fwd, 0)

    def bwd(i, carry):
        c = n_chunks - 1 - i
        off = pl.multiple_of(c * c_len, c_len)
        q = q_ref[0, pl.ds(off, c_len), :]
        k = k_ref[0, pl.ds(off, c_len), :]
        v = v_ref[0, pl.ds(off, c_len), :]
        s = sb_scr[...]
        y = y_scr[pl.ds(off, c_len), :] + _dot(q, s.astype(BF16)) * dq_b
        mu = jnp.mean(y, axis=-1, keepdims=True)
        yc = y - mu
        var = jnp.mean(yc * yc, axis=-1, keepdims=True)
        gate = jax.nn.silu(g_ref[0, pl.ds(off, c_len), :].astype(F32))
        o_ref[0, pl.ds(off, c_len), :] = (gate * (yc * lax.rsqrt(var + LN_EPS))).astype(o_ref.dtype)
        sb_scr[...] = s * ds_b + _dot_tn((k.astype(F32) * dk_b).astype(BF16), v)
        return carry

    lax.fori_loop(0, n_chunks, bwd, 0)


def _ret_call(lg_tab, qk, vg, krc, vrc):
    bsz, s, _ = qk.shape
    p_len = krc.shape[1]
    chunk = min(RET_CHUNK, s)
    assert s % chunk == 0
    return pl.pallas_call(
        functools.partial(_ret_kernel, chunk=chunk, n_chunks=s // chunk),
        out_shape=jax.ShapeDtypeStruct((bsz, s, RET_HEADS * RET_DV), BF16),
        grid=(bsz, RET_HEADS),
        in_specs=[pl.BlockSpec((1, 8, LANES), lambda b, h: (h, 0, 0)),
                  pl.BlockSpec((1, s, RET_DK), lambda b, h: (b, 0, h)),
                  pl.BlockSpec((1, s, RET_DK), lambda b, h: (b, 0, RET_HEADS + h)),
                  pl.BlockSpec((1, s, RET_DV), lambda b, h: (b, 0, h)),
                  pl.BlockSpec((1, s, RET_DV), lambda b, h: (b, 0, RET_HEADS + h)),
                  pl.BlockSpec((1, p_len, RET_DK), lambda b, h: (b, 0, h)),
                  pl.BlockSpec((1, p_len, RET_DV), lambda b, h: (b, 0, h))],
        out_specs=pl.BlockSpec((1, s, RET_DV), lambda b, h: (b, 0, h)),
        scratch_shapes=[pltpu.VMEM((s, RET_DV), F32),
                        pltpu.VMEM((RET_DK, RET_DV), F32),
                        pltpu.VMEM((RET_DK, RET_DV), F32)],
        compiler_params=_cparams(("arbitrary", "arbitrary")),
        name="retention",
    )(lg_tab, qk, qk, vg, vg, krc, vrc)


def _diff_kernel(dl_ref, gsub_ref, q_ref, k_ref, v_ref, kc_ref, vc_ref, o_ref):
    dl = dl_ref[...]
    lam = (jnp.exp(jnp.sum(dl[0:1] * dl[1:2], axis=-1, keepdims=True))
           - jnp.exp(jnp.sum(dl[2:3] * dl[3:4], axis=-1, keepdims=True)) + LAM_INIT)
    q = q_ref[0]
    lane = lax.broadcasted_iota(jnp.int32, q.shape, 1)
    comp0 = (lane & (LANES // 4)) == 0
    zero = jnp.zeros_like(q)
    k = k_ref[0]
    kc = kc_ref[0]

    def softmax_parts(qc):
        s_lat = _dot_nt(qc, k)
        s_ctx = _dot_nt(qc, kc)
        m = jnp.maximum(jnp.max(s_lat, axis=-1, keepdims=True), jnp.max(s_ctx, axis=-1, keepdims=True))
        p_lat = jnp.exp(s_lat - m)
        p_ctx = jnp.exp(s_ctx - m)
        l = jnp.sum(p_lat, axis=-1, keepdims=True) + jnp.sum(p_ctx, axis=-1, keepdims=True)
        return p_lat, p_ctx, l

    p0_lat, p0_ctx, l0 = softmax_parts(jnp.where(comp0, q, zero))
    p1_lat, p1_ctx, l1 = softmax_parts(jnp.where(comp0, zero, q))
    r0 = 1.0 / l0
    r1 = lam / l1
    a_lat = (p0_lat * r0 - p1_lat * r1).astype(BF16)
    a_ctx = (p0_ctx * r0 - p1_ctx * r1).astype(BF16)
    o = _dot(a_lat, v_ref[0]) + _dot(a_ctx, vc_ref[0])
    o = o * lax.rsqrt(jnp.mean(o * o, axis=-1, keepdims=True) + LN_EPS) * gsub_ref[...] * (1.0 - LAM_INIT)
    o_ref[0] = o.astype(o_ref.dtype)


def _diff_call(dl, gsub, qk, vgg, kdc, vdc, *, tq=256):
    bsz, s, _ = qk.shape
    p_len = kdc.shape[1]
    tq = min(tq, s)
    hd = 2 * DIFF_DH
    return pl.pallas_call(
        _diff_kernel,
        out_shape=jax.ShapeDtypeStruct((bsz, s, D_MODEL), BF16),
        grid=(bsz, DIFF_HEADS, s // tq),
        in_specs=[pl.BlockSpec((4, DIFF_DH), lambda b, h, i: (0, 0)),
                  pl.BlockSpec((1, hd), lambda b, h, i: (0, 0)),
                  pl.BlockSpec((1, tq, hd), lambda b, h, i: (b, i, h)),
                  pl.BlockSpec((1, s, hd), lambda b, h, i: (b, 0, DIFF_HEADS + h)),
                  pl.BlockSpec((1, s, hd), lambda b, h, i: (b, 0, h)),
                  pl.BlockSpec((1, p_len, hd), lambda b, h, i: (b, 0, h)),
                  pl.BlockSpec((1, p_len, hd), lambda b, h, i: (b, 0, h))],
        out_specs=pl.BlockSpec((1, tq, hd), lambda b, h, i: (b, i, h)),
        compiler_params=_cparams(("arbitrary", "arbitrary", "arbitrary")),
        name="diff_attn",
    )(dl, gsub, qk, qk, vgg, kdc, vdc)


def _merge_kernel(zr_ref, zd_ref, gr_ref, gd_ref, xn_ref, mod_ref, bg_ref, wr_ref, wd_ref, wo_ref, g1_ref, b1_ref,
                  x1_ref, h2_ref):
    y_ret = _dot(zr_ref[0], wr_ref[...])
    y_dif = _dot(zd_ref[0], wd_ref[...])
    bg = bg_ref[...]
    gate_r = jax.nn.sigmoid(gr_ref[0].astype(F32) + bg[0:1])
    gate_d = jax.nn.sigmoid(gd_ref[0].astype(F32) + bg[1:2])
    y_mix = _dot((gate_r * y_ret + gate_d * y_dif).astype(BF16), wo_ref[...])
    m = mod_ref[0]
    x1 = _layer_norm(ALPHA * xn_ref[0] + m[2:3] * y_mix, g1_ref[...], b1_ref[...])
    x1_ref[0] = x1
    h2_ref[0] = (x1 * (1.0 + m[4:5]) + m[3:4]).astype(BF16)


def _merge_call(z_ret, z_dif, vgg, xn, mod, bg, w_ret, w_dif, w_o, g1, b1, *, tm=512):
    bsz, s, d = xn.shape
    tm = min(tm, s)
    const = lambda b, i: (0, 0)
    return pl.pallas_call(
        _merge_kernel,
        out_shape=[jax.ShapeDtypeStruct((bsz, s, d), F32), jax.ShapeDtypeStruct((bsz, s, d), BF16)],
        grid=(bsz, s // tm),
        in_specs=[pl.BlockSpec((1, tm, RET_HEADS * RET_DV), lambda b, i: (b, i, 0)),
                  pl.BlockSpec((1, tm, d), lambda b, i: (b, i, 0)),
                  pl.BlockSpec((1, tm, d), lambda b, i: (b, i, 1)),
                  pl.BlockSpec((1, tm, d), lambda b, i: (b, i, 2)),
                  pl.BlockSpec((1, tm, d), lambda b, i: (b, i, 0)),
                  pl.BlockSpec((1, 6, d), lambda b, i: (b, 0, 0)),
                  pl.BlockSpec((2, d), const),
                  pl.BlockSpec(w_ret.shape, const),
                  pl.BlockSpec(w_dif.shape, const),
                  pl.BlockSpec(w_o.shape, const),
                  pl.BlockSpec((1, d), const),
                  pl.BlockSpec((1, d), const)],
        out_specs=[pl.BlockSpec((1, tm, d), lambda b, i: (b, i, 0)),
                   pl.BlockSpec((1, tm, d), lambda b, i: (b, i, 0))],
        compiler_params=_cparams(("arbitrary", "arbitrary")),
        name="merge",
    )(z_ret, z_dif, vgg, vgg, xn, mod, bg, w_ret, w_dif, w_o, g1, b1)


HALO = 16
FF_CHUNK = 256


def _ffn_down_kernel(u_ref, up_ref, un_ref, gt_ref, cw_ref, cb_ref, wd_ref, x1_ref, mod_ref, g2_ref, b2_ref, o_ref,
                     *, tm, n_tiles):
    i = pl.program_id(1)
    row = lax.broadcasted_iota(jnp.int32, (tm, FF_CHUNK), 0)
    acc = jnp.zeros((tm, D_MODEL), F32)
    for j in range(D_FF // FF_CHUNK):
        sl = slice(j * FF_CHUNK, (j + 1) * FF_CHUNK)
        u = u_ref[0, :, sl].astype(F32)
        prev_row = jnp.where(i > 0, up_ref[0, :, sl].astype(F32)[HALO - 1:HALO], 0.0)
        next_row = jnp.where(i < n_tiles - 1, un_ref[0, :, sl].astype(F32)[0:1], 0.0)
        u_m1 = jnp.where(row == 0, prev_row, pltpu.roll(u, 1, axis=0))
        u_p1 = jnp.where(row == tm - 1, next_row, pltpu.roll(u, tm - 1, axis=0))
        cw = cw_ref[:, sl]
        t = cb_ref[:, sl] + u_m1 * cw[0:1] + u * cw[1:2] + u_p1 * cw[2:3]
        act = 0.5 * t * (1.0 + lax.erf(t * np.float32(np.sqrt(0.5)))) * gt_ref[0, :, sl].astype(F32)
        acc = acc + _dot(act.astype(BF16), wd_ref[sl, :])
    m = mod_ref[0]
    o_ref[0] = _layer_norm(ALPHA * x1_ref[0] + m[5:6] * acc, g2_ref[...], b2_ref[...])


def _ffn_down_call(ug, conv_w, conv_b, w_down, x1, mod, g2, b2, *, tm=512):
    bsz, s, d = x1.shape
    tm = min(tm, s)
    n_tiles = s // tm
    hb = tm // HALO
    n_halo = s // HALO
    const = lambda b, i: (0, 0)
    return pl.pallas_call(
        functools.partial(_ffn_down_kernel, tm=tm, n_tiles=n_tiles),
        out_shape=jax.ShapeDtypeStruct((bsz, s, d), F32),
        grid=(bsz, n_tiles),
        in_specs=[pl.BlockSpec((1, tm, D_FF), lambda b, i: (b, i, 0)),
                  pl.BlockSpec((1, HALO, D_FF), lambda b, i: (b, jnp.maximum(i * hb - 1, 0), 0)),
                  pl.BlockSpec((1, HALO, D_FF), lambda b, i: (b, jnp.minimum((i + 1) * hb, n_halo - 1), 0)),
                  pl.BlockSpec((1, tm, D_FF), lambda b, i: (b, i, 1)),
                  pl.BlockSpec((CONV_W, D_FF), const),
                  pl.BlockSpec((1, D_FF), const),
                  pl.BlockSpec((D_FF, d), const),
                  pl.BlockSpec((1, tm, d), lambda b, i: (b, i, 0)),
                  pl.BlockSpec((1, 6, d), lambda b, i: (b, 0, 0)),
                  pl.BlockSpec((1, d), const),
                  pl.BlockSpec((1, d), const)],
        out_specs=pl.BlockSpec((1, tm, d), lambda b, i: (b, i, 0)),
        compiler_params=_cparams(("arbitrary", "arbitrary")),
        name="ffn_down",
    )(ug, ug, ug, ug, conv_w, conv_b, w_down, x1, mod, g2, b2)


def _axial_angles(s, head_dim):
    rows = s // GRID_W
    row = jnp.repeat(jnp.arange(rows, dtype=F32), GRID_W)
    col = jnp.tile(jnp.arange(GRID_W, dtype=F32), rows)
    half = head_dim // 2
    inv = ROPE_BASE ** (-(jnp.arange(0, half, 2, dtype=F32) / half))
    return jnp.concatenate([row[:, None] * inv, col[:, None] * inv], axis=-1)


def _ret_tables(s):
    ang = _axial_angles(s, RET_DK)
    cos, sin = jnp.cos(ang), jnp.sin(ang)
    k_scale = RET_DK ** -0.5
    return jnp.stack([cos, cos * k_scale]), jnp.stack([sin, sin * k_scale])


def _dif_tables(s):
    ang = _axial_angles(s, DIFF_DH)
    cos, sin = jnp.cos(ang), jnp.sin(ang)
    cos_t = jnp.tile(cos, (1, 4))
    sin_t = jnp.concatenate([-sin, -sin, sin, sin], axis=-1)
    q_scale = DIFF_DH ** -0.5
    return jnp.stack([cos_t * q_scale, cos_t]), jnp.stack([sin_t * q_scale, sin_t])


def _dif_col_perm():
    q4 = DIFF_DH // 2
    g = np.concatenate([np.arange(0, q4), np.arange(2 * q4, 3 * q4), np.arange(q4, 2 * q4), np.arange(3 * q4, 4 * q4)])
    return np.concatenate([h * LANES + g for h in range(2 * DIFF_HEADS)])


def kernel(x, c, ctx, c_ctx, ln_in_g, ln_in_b, w_mod, b_mod, w_in, b_gate, ret_decay_logit, diff_lambda, diff_subln_g,
           w_ret_out, w_diff_out, w_o, ln1_g, ln1_b, w_up, conv_w, conv_b, w_down, ln2_g, ln2_b):
    bsz, s, d = x.shape
    p_len = ctx.shape[1]
    row2 = lambda v: v.reshape(1, -1)

    n_rows = ((bsz + 1 + 7) // 8) * 8
    cc = jnp.zeros((n_rows, d), F32).at[:bsz].set(c).at[bsz].set(c_ctx)
    mod = _mod_call(cc, w_mod[0], row2(b_mod[0])).reshape(n_rows, 6, d)

    xn, h1 = _ln_mod_call(x, row2(ln_in_g), row2(ln_in_b), mod, mod_row=None, emit_xn=True, ts=512)
    (hc,) = _ln_mod_call(ctx, row2(ln_in_g), row2(ln_in_b), mod, mod_row=bsz, emit_xn=False, ts=256)

    w = w_in[0]
    w_qkd = w[:, OFF_QD:OFF_VD][:, _dif_col_perm()]
    w_bf = jnp.concatenate([w[:, :OFF_QD], w_qkd, w[:, OFF_VD:]], axis=1).astype(BF16)

    h1f = h1.reshape(bsz * s, d)
    hcf = hc.reshape(bsz * p_len, d)
    qk_r = _proj_call(h1f, w_bf, col0=OFF_QR, ncols=OFF_VR - OFF_QR, out_dtype=BF16, epi="rope_ret",
                      tables=_ret_tables(s), seq=s, name="proj_qk_ret").reshape(bsz, s, -1)
    vg_r = _proj_call(h1f, w_bf, col0=OFF_VR, ncols=OFF_QD - OFF_VR, out_dtype=BF16,
                      name="proj_vg_ret").reshape(bsz, s, -1)
    qk_d = _proj_call(h1f, w_bf, col0=OFF_QD, ncols=OFF_VD - OFF_QD, out_dtype=BF16, epi="rope_dif",
                      tables=_dif_tables(s), seq=s, name="proj_qk_dif").reshape(bsz, s, -1)
    vgg = _proj_call(h1f, w_bf, col0=OFF_VD, ncols=N_IN - OFF_VD, out_dtype=BF16,
                     name="proj_v_gates").reshape(bsz, s, -1)
    kr_c = _proj_call(hcf, w_bf, col0=OFF_KR, ncols=D_MODEL, out_dtype=F32, scale=RET_DK ** -0.5,
                      name="proj_ctx_kr").reshape(bsz, p_len, -1)
    vr_c = _proj_call(hcf, w_bf, col0=OFF_VR, ncols=RET_HEADS * RET_DV, out_dtype=BF16,
                      name="proj_ctx_vr").reshape(bsz, p_len, -1)
    kd_c = _proj_call(hcf, w_bf, col0=OFF_KD, ncols=D_MODEL, out_dtype=BF16,
                      name="proj_ctx_kd").reshape(bsz, p_len, -1)
    vd_c = _proj_call(hcf, w_bf, col0=OFF_VD, ncols=D_MODEL, out_dtype=BF16,
                      name="proj_ctx_vd").reshape(bsz, p_len, -1)

    log_g = jax.nn.log_sigmoid(ret_decay_logit[0].astype(F32))
    lg_tab = jnp.zeros((RET_HEADS, 8, LANES), F32).at[:, :2, :].set(
        jnp.broadcast_to(log_g.T[:, :, None], (RET_HEADS, 2, LANES)))
    z_ret = _ret_call(lg_tab, qk_r, vg_r, kr_c, vr_c)
    z_dif = _diff_call(diff_lambda[0].astype(F32), row2(diff_subln_g[0]), qk_d, vgg, kd_c, vd_c)

    x1, h2 = _merge_call(z_ret, z_dif, vgg, xn, mod, b_gate[0].reshape(2, d), w_ret_out[0].astype(BF16),
                         w_diff_out[0].astype(BF16), w_o[0].astype(BF16), row2(ln1_g[0]), row2(ln1_b[0]))

    ug = _proj_call(h2.reshape(bsz * s, d), w_up[0].astype(BF16), col0=0, ncols=2 * D_FF, out_dtype=BF16,
                    tm=512, tn=D_FF, name="ffn_up").reshape(bsz, s, -1)
    return _ffn_down_call(ug, conv_w[0], row2(conv_b[0]), w_down[0].astype(BF16), x1, mod, row2(ln2_g[0]),
                          row2(ln2_b[0]))
```

```python
import functools
import math

import numpy as np
import jax
import jax.numpy as jnp
from jax import lax
from jax.experimental import pallas as pl
from jax.experimental.pallas import tpu as pltpu

F32 = jnp.float32
BF16 = jnp.bfloat16

D_MODEL = 1024
GRID_W = 64
RET_HEADS = 4
RET_DK = D_MODEL // RET_HEADS
RET_DV = 2 * RET_DK
DIFF_DH = 64
DIFF_HEADS = D_MODEL // (2 * DIFF_DH)
D_FF = ((8 * D_MODEL // 3 + 127) // 128) * 128
CONV_W = 3
ROPE_BASE = 10000.0
LN_EPS = 1e-5
DEPTH = 1
ALPHA = (2.0 * DEPTH) ** 0.25
LAM_INIT = 0.8 - 0.6 * math.exp(-0.3 * 0)

OFF_QR, OFF_KR, OFF_VR, OFF_GR = 0, RET_HEADS * RET_DK, 2 * RET_HEADS * RET_DK, 2 * RET_HEADS * RET_DK + RET_HEADS * RET_DV
OFF_QD = OFF_GR + RET_HEADS * RET_DV
OFF_KD = OFF_QD + D_MODEL
OFF_VD = OFF_KD + D_MODEL
OFF_GATE_R = OFF_VD + D_MODEL
OFF_GATE_D = OFF_GATE_R + D_MODEL
N_IN = OFF_GATE_D + D_MODEL

LANES = 128
RET_CHUNK = 256
VMEM_LIMIT = 60 * 1024 * 1024


def _cparams(sem):
    return pltpu.CompilerParams(dimension_semantics=sem, vmem_limit_bytes=VMEM_LIMIT)


def _layer_norm(x, g, b):
    mu = jnp.mean(x, axis=-1, keepdims=True)
    xc = x - mu
    var = jnp.mean(xc * xc, axis=-1, keepdims=True)
    return xc * lax.rsqrt(var + LN_EPS) * g + b


def _dot(a, b):
    return jnp.dot(a, b, preferred_element_type=F32)


def _dot_nt(a, b):
    return lax.dot_general(a, b, (((1,), (1,)), ((), ())), preferred_element_type=F32)


def _dot_tn(a, b):
    return lax.dot_general(a, b, (((0,), (0,)), ((), ())), preferred_element_type=F32)


def _mod_kernel(c_ref, w_ref, b_ref, o_ref):
    cond = jax.nn.silu(c_ref[...])
    o_ref[...] = _dot(cond.astype(BF16), w_ref[...].astype(BF16)) + b_ref[...]


def _mod_call(cc, w_mod, b_mod):
    rows, d = cc.shape
    n = w_mod.shape[1]
    tn = 1024
    return pl.pallas_call(
        _mod_kernel,
        out_shape=jax.ShapeDtypeStruct((rows, n), F32),
        grid=(n // tn,),
        in_specs=[pl.BlockSpec((rows, d), lambda j: (0, 0)),
                  pl.BlockSpec((d, tn), lambda j: (0, j)),
                  pl.BlockSpec((1, tn), lambda j: (0, j))],
        out_specs=pl.BlockSpec((rows, tn), lambda j: (0, j)),
        compiler_params=_cparams(("arbitrary",)),
        name="mod",
    )(cc, w_mod, b_mod)


def _ln_mod_kernel(x_ref, g_ref, b_ref, mod_ref, *out_refs, emit_xn):
    xn = _layer_norm(x_ref[0], g_ref[...], b_ref[...])
    m = mod_ref[0]
    h = (xn * (1.0 + m[1:2]) + m[0:1]).astype(BF16)
    if emit_xn:
        out_refs[0][0] = xn
        out_refs[1][0] = h
    else:
        out_refs[0][0] = h


def _ln_mod_call(x, g, b, mod, *, mod_row, emit_xn, ts):
    bsz, s, d = x.shape
    ts = min(ts, s)
    mod_map = (lambda i, j: (i, 0, 0)) if mod_row is None else (lambda i, j: (mod_row, 0, 0))
    out_shape = [jax.ShapeDtypeStruct((bsz, s, d), BF16)]
    out_specs = [pl.BlockSpec((1, ts, d), lambda i, j: (i, j, 0))]
    if emit_xn:
        out_shape = [jax.ShapeDtypeStruct((bsz, s, d), F32)] + out_shape
        out_specs = [pl.BlockSpec((1, ts, d), lambda i, j: (i, j, 0))] + out_specs
    return pl.pallas_call(
        functools.partial(_ln_mod_kernel, emit_xn=emit_xn),
        out_shape=out_shape,
        grid=(bsz, s // ts),
        in_specs=[pl.BlockSpec((1, ts, d), lambda i, j: (i, j, 0)),
                  pl.BlockSpec((1, d), lambda i, j: (0, 0)),
                  pl.BlockSpec((1, d), lambda i, j: (0, 0)),
                  pl.BlockSpec((1, 6, d), mod_map)],
        out_specs=out_specs,
        compiler_params=_cparams(("arbitrary", "arbitrary")),
        name="ln_mod_x" if emit_xn else "ln_mod_ctx",
    )(x, g, b, mod)


def _proj_kernel(x_ref, w_ref, *rest, epi, scale, tn):
    acc = _dot(x_ref[...], w_ref[...])
    if epi == "plain":
        (o_ref,) = rest
        if scale != 1.0:
            acc = acc * scale
        o_ref[...] = acc.astype(o_ref.dtype)
    elif epi == "rope_ret":
        cos_ref, sin_ref, o_ref = rest
        cos, sin = cos_ref[0], sin_ref[0]
        for h in range(tn // (2 * LANES)):
            a = h * 2 * LANES
            x1 = acc[:, a:a + LANES]
            x2 = acc[:, a + LANES:a + 2 * LANES]
            o_ref[:, a:a + LANES] = (x1 * cos - x2 * sin).astype(o_ref.dtype)
            o_ref[:, a + LANES:a + 2 * LANES] = (x2 * cos + x1 * sin).astype(o_ref.dtype)
    else:
        cos_ref, sin_ref, o_ref = rest
        cos, sin = cos_ref[0], sin_ref[0]
        for g in range(tn // LANES):
            a = g * LANES
            xg = acc[:, a:a + LANES]
            o_ref[:, a:a + LANES] = (xg * cos + pltpu.roll(xg, LANES // 2, axis=1) * sin).astype(o_ref.dtype)


def _proj_call(x, w, *, col0, ncols, out_dtype, epi="plain", scale=1.0, tables=None, seq=None, tm=1024, tn=1024, name="proj"):
    m, k = x.shape
    tm = min(tm, m if seq is None else seq)
    tn = min(tn, ncols)
    assert m % tm == 0 and ncols % tn == 0 and col0 % tn == 0
    jb = col0 // tn
    in_specs = [pl.BlockSpec((tm, k), lambda j, i: (i, 0)),
                pl.BlockSpec((k, tn), lambda j, i: (0, jb + j))]
    args = [x, w]
    if epi != "plain":
        assert seq % tm == 0
        ns = seq // tm
        tspec = pl.BlockSpec((1, tm, LANES), lambda j, i: (j, i % ns, 0))
        in_specs += [tspec, tspec]
        args += list(tables)
    return pl.pallas_call(
        functools.partial(_proj_kernel, epi=epi, scale=scale, tn=tn),
        out_shape=jax.ShapeDtypeStruct((m, ncols), out_dtype),
        grid=(ncols // tn, m // tm),
        in_specs=in_specs,
        out_specs=pl.BlockSpec((tm, tn), lambda j, i: (i, j)),
        compiler_params=_cparams(("arbitrary", "arbitrary")),
        name=name,
    )(*args)


def _ret_kernel(lg_ref, q_ref, k_ref, v_ref, g_ref, kc_ref, vc_ref, o_ref, y_scr, sf_scr, sb_scr, *, chunk, n_chunks):
    c_len = chunk
    lg = lg_ref[0]
    lgf = lg[0:1, :]
    lgb = lg[1:2, :]

    def col(x):
        return x[:, 0:1]

    p_len = kc_ref.shape[1]
    pos_p = lax.broadcasted_iota(jnp.int32, (p_len, LANES), 0).astype(F32)
    kc = kc_ref[0]
    vc = vc_ref[0]
    sf_scr[...] = _dot_tn((kc * col(jnp.exp(((p_len - 1.0) - pos_p) * lgf))).astype(BF16), vc)
    sb_scr[...] = _dot_tn((kc * col(jnp.exp(pos_p * lgb))).astype(BF16), vc)

    pos = lax.broadcasted_iota(jnp.int32, (c_len, LANES), 0).astype(F32)
    dq_f = col(jnp.exp((pos + 1.0) * lgf))
    dk_f = col(jnp.exp((c_len - 1.0 - pos) * lgf))
    dq_b = col(jnp.exp((c_len - pos) * lgb))
    dk_b = col(jnp.exp(pos * lgb))
    ds_f = col(jnp.exp(c_len * lgf))
    ds_b = col(jnp.exp(c_len * lgb))
    rel = (lax.broadcasted_iota(jnp.int32, (c_len, c_len), 0)
           - lax.broadcasted_iota(jnp.int32, (c_len, c_len), 1)).astype(F32)
    d_in = (jnp.where(rel >= 0, jnp.exp(jnp.maximum(rel, 0.0) * col(lgf)), 0.0)
            + jnp.where(rel <= 0, jnp.exp(jnp.maximum(-rel, 0.0) * col(lgb)), 0.0))

    def fwd(c, carry):
        off = pl.multiple_of(c * c_len, c_len)
        q = q_ref[0, pl.ds(off, c_len), :]
        k = k_ref[0, pl.ds(off, c_len), :]
        v = v_ref[0, pl.ds(off, c_len), :]
        inner = _dot_nt(q, k) * d_in
        s = sf_scr[...]
        y_scr[pl.ds(off, c_len), :] = _dot(inner.astype(BF16), v) + _dot(q, s.astype(BF16)) * dq_f
        sf_scr[...] = s * ds_f + _dot_tn((k.astype(F32) * dk_f).astype(BF16), v)
        return carry

    lax.fori_loop(0, n_chunks, fwd, 0)

    def bwd(i, carry):
        c = n_chunks - 1 - i
        off = pl.multiple_of(c * c_len, c_len)
        q = q_ref[0, pl.ds(off, c_len), :]
        k = k_ref[0, pl.ds(off, c_len), :]
        v = v_ref[0, pl.ds(off, c_len), :]
        s = sb_scr[...]
        y = y_scr[pl.ds(off, c_len), :] + _dot(q, s.astype(BF16)) * dq_b
        mu = jnp.mean(y, axis=-1, keepdims=True)
        yc = y - mu
        var = jnp.mean(yc * yc, axis=-1, keepdims=True)
        gate = jax.nn.silu(g_ref[0, pl.ds(off, c_len), :].astype(F32))
        o_ref[0, pl.ds(off, c_len), :] = (gate * (yc * lax.rsqrt(var + LN_EPS))).astype(o_ref.dtype)
        sb_scr[...] = s * ds_b + _dot_tn((k.astype(F32) * dk_b).astype(BF16), v)
        return carry

    lax.fori_loop(0, n_chunks, bwd, 0)


def _ret_call(lg_tab, qk, vg, krc, vrc):
    bsz, s, _ = qk.shape
    p_len = krc.shape[1]
    chunk = min(RET_CHUNK, s)
    assert s % chunk == 0
    return pl.pallas_call(
        functools.partial(_ret_kernel, chunk=chunk, n_chunks=s // chunk),
        out_shape=jax.ShapeDtypeStruct((bsz, s, RET_HEADS * RET_DV), BF16),
        grid=(bsz, RET_HEADS),
        in_specs=[pl.BlockSpec((1, 8, LANES), lambda b, h: (h, 0, 0)),
                  pl.BlockSpec((1, s, RET_DK), lambda b, h: (b, 0, h)),
                  pl.BlockSpec((1, s, RET_DK), lambda b, h: (b, 0, RET_HEADS + h)),
                  pl.BlockSpec((1, s, RET_DV), lambda b, h: (b, 0, h)),
                  pl.BlockSpec((1, s, RET_DV), lambda b, h: (b, 0, RET_HEADS + h)),
                  pl.BlockSpec((1, p_len, RET_DK), lambda b, h: (b, 0, h)),
                  pl.BlockSpec((1, p_len, RET_DV), lambda b, h: (b, 0, h))],
        out_specs=pl.BlockSpec((1, s, RET_DV), lambda b, h: (b, 0, h)),
        scratch_shapes=[pltpu.VMEM((s, RET_DV), F32),
                        pltpu.VMEM((RET_DK, RET_DV), F32),
                        pltpu.VMEM((RET_DK, RET_DV), F32)],
        compiler_params=_cparams(("arbitrary", "arbitrary")),
        name="retention",
    )(lg_tab, qk, qk, vg, vg, krc, vrc)


DIFF_TQI = 128
DIFF_TK = 256
ONES_ROWS = 16
NEG_BIG = -1e30


def _diff_kernel(dl_ref, gsub_ref, q_ref, k_ref, v_ref, kc_ref, vc_ref, o_ref, vt_scr, sa_scr, sb_scr, q2_scr,
                 *, s_len, p_len):
    tk = DIFF_TK
    n_lat, n_ctx = s_len // tk, p_len // tk
    n_blk = n_lat + n_ctx

    tqi = DIFF_TQI
    hd = 2 * DIFF_DH
    n_sub = s_len // tqi

    ones = jnp.ones((ONES_ROWS, tk), BF16)
    for j in range(n_blk):
        src = v_ref[0, j * tk:(j + 1) * tk, :] if j < n_lat else vc_ref[0, (j - n_lat) * tk:(j - n_lat + 1) * tk, :]
        vt_scr[j, 0:hd, :] = src.astype(F32).T.astype(BF16)
        vt_scr[j, hd:, :] = ones

    feat = lax.broadcasted_iota(jnp.int32, (hd, tqi), 0)
    comp0 = (feat & (hd // 4)) == 0

    def prep_q(t, carry):
        qt = q_ref[0, pl.ds(pl.multiple_of(t * tqi, tqi), tqi), :].astype(F32).T
        q2_scr[t] = jnp.concatenate([jnp.where(comp0, qt, 0.0), jnp.where(comp0, 0.0, qt)], axis=1).astype(BF16)
        return carry

    lax.fori_loop(0, n_sub, prep_q, 0, unroll=4)

    dl = dl_ref[...]
    lam = (jnp.exp(jnp.sum(dl[0:1] * dl[1:2], axis=-1, keepdims=True))
           - jnp.exp(jnp.sum(dl[2:3] * dl[3:4], axis=-1, keepdims=True)) + LAM_INIT)
    gsub = gsub_ref[...] * (1.0 - LAM_INIT)

    def key_block(j):
        return k_ref[0, j * tk:(j + 1) * tk, :] if j < n_lat else kc_ref[0, (j - n_lat) * tk:(j - n_lat + 1) * tk, :]

    def stage(t_val, s_val, mx, t_score, s_score):
        mx_next = jnp.full((8, 2 * tqi), NEG_BIG, F32)
        if t_val is not None:
            m = jnp.max(mx, axis=0, keepdims=True)
            acc = jnp.zeros((hd + ONES_ROWS, 2 * tqi), F32)
        if t_score is not None:
            q2 = q2_scr[t_score]
        for j in range(n_blk):
            if t_val is not None:
                p = jnp.exp(s_val[j] - m)
                acc = acc + _dot(vt_scr[j], p.astype(BF16))
            if t_score is not None:
                s = _dot(key_block(j), q2)
                s_score[j] = s
                mx_next = jnp.maximum(mx_next, jnp.max(s.reshape(tk // 8, 8, 2 * tqi), axis=0))
        if t_val is not None:
            o2 = acc[0:hd] * (1.0 / acc[hd:hd + 1])
            o = (o2[:, :tqi] - lam * o2[:, tqi:]).T
            o = o * lax.rsqrt(jnp.mean(o * o, axis=-1, keepdims=True) + LN_EPS) * gsub
            row0 = t_val * tqi if isinstance(t_val, int) else pl.multiple_of(t_val * tqi, tqi)
            o_ref[0, pl.ds(row0, tqi), :] = o.astype(o_ref.dtype)
        return mx_next

    mx = stage(None, None, None, 0, sa_scr)

    def stage_pair(u, mx):
        mx = stage(2 * u, sa_scr, mx, 2 * u + 1, sb_scr)
        return stage(2 * u + 1, sb_scr, mx, 2 * u + 2, sa_scr)

    mx = lax.fori_loop(0, n_sub // 2 - 1, stage_pair, mx)
    mx = stage(n_sub - 2, sa_scr, mx, n_sub - 1, sb_scr)
    stage(n_sub - 1, sb_scr, mx, None, None)


def _diff_call(dl, gsub, qk, vgg, kdc, vdc):
    bsz, s, _ = qk.shape
    p_len = kdc.shape[1]
    hd = 2 * DIFF_DH
    n_blk = (s + p_len) // DIFF_TK
    assert s % DIFF_TK == 0 and p_len % DIFF_TK == 0 and s % (2 * DIFF_TQI) == 0
    return pl.pallas_call(
        functools.partial(_diff_kernel, s_len=s, p_len=p_len),
        out_shape=jax.ShapeDtypeStruct((bsz, s, D_MODEL), BF16),
        scratch_shapes=[pltpu.VMEM((n_blk, hd + ONES_ROWS, DIFF_TK), BF16),
                        pltpu.VMEM((n_blk, DIFF_TK, 2 * DIFF_TQI), F32),
                        pltpu.VMEM((n_blk, DIFF_TK, 2 * DIFF_TQI), F32),
                        pltpu.VMEM((s // DIFF_TQI, hd, 2 * DIFF_TQI), BF16)],
        grid=(bsz, DIFF_HEADS),
        in_specs=[pl.BlockSpec((4, DIFF_DH), lambda b, h: (0, 0)),
                  pl.BlockSpec((1, hd), lambda b, h: (0, 0)),
                  pl.BlockSpec((1, s, hd), lambda b, h: (b, 0, h)),
                  pl.BlockSpec((1, s, hd), lambda b, h: (b, 0, DIFF_HEADS + h)),
                  pl.BlockSpec((1, s, hd), lambda b, h: (b, 0, h)),
                  pl.BlockSpec((1, p_len, hd), lambda b, h: (b, 0, h)),
                  pl.BlockSpec((1, p_len, hd), lambda b, h: (b, 0, h))],
        out_specs=pl.BlockSpec((1, s, hd), lambda b, h: (b, 0, h)),
        compiler_params=_cparams(("arbitrary", "arbitrary")),
        name="diff_attn",
    )(dl, gsub, qk, qk, vgg, kdc, vdc)


def _merge_kernel(zr_ref, zd_ref, gr_ref, gd_ref, xn_ref, mod_ref, bg_ref, wr_ref, wd_ref, wo_ref, g1_ref, b1_ref,
                  x1_ref, h2_ref):
    y_ret = _dot(zr_ref[0], wr_ref[...])
    y_dif = _dot(zd_ref[0], wd_ref[...])
    bg = bg_ref[...]
    gate_r = jax.nn.sigmoid(gr_ref[0].astype(F32) + bg[0:1])
    gate_d = jax.nn.sigmoid(gd_ref[0].astype(F32) + bg[1:2])
    y_mix = _dot((gate_r * y_ret + gate_d * y_dif).astype(BF16), wo_ref[...])
    m = mod_ref[0]
    x1 = _layer_norm(ALPHA * xn_ref[0] + m[2:3] * y_mix, g1_ref[...], b1_ref[...])
    x1_ref[0] = x1
    h2_ref[0] = (x1 * (1.0 + m[4:5]) + m[3:4]).astype(BF16)


def _merge_call(z_ret, z_dif, vgg, xn, mod, bg, w_ret, w_dif, w_o, g1, b1, *, tm=512):
    bsz, s, d = xn.shape
    tm = min(tm, s)
    const = lambda b, i: (0, 0)
    return pl.pallas_call(
        _merge_kernel,
        out_shape=[jax.ShapeDtypeStruct((bsz, s, d), F32), jax.ShapeDtypeStruct((bsz, s, d), BF16)],
        grid=(bsz, s // tm),
        in_specs=[pl.BlockSpec((1, tm, RET_HEADS * RET_DV), lambda b, i: (b, i, 0)),
                  pl.BlockSpec((1, tm, d), lambda b, i: (b, i, 0)),
                  pl.BlockSpec((1, tm, d), lambda b, i: (b, i, 1)),
                  pl.BlockSpec((1, tm, d), lambda b, i: (b, i, 2)),
                  pl.BlockSpec((1, tm, d), lambda b, i: (b, i, 0)),
                  pl.BlockSpec((1, 6, d), lambda b, i: (b, 0, 0)),
                  pl.BlockSpec((2, d), const),
                  pl.BlockSpec(w_ret.shape, const),
                  pl.BlockSpec(w_dif.shape, const),
                  pl.BlockSpec(w_o.shape, const),
                  pl.BlockSpec((1, d), const),
                  pl.BlockSpec((1, d), const)],
        out_specs=[pl.BlockSpec((1, tm, d), lambda b, i: (b, i, 0)),
                   pl.BlockSpec((1, tm, d), lambda b, i: (b, i, 0))],
        compiler_params=_cparams(("arbitrary", "arbitrary")),
        name="merge",
    )(z_ret, z_dif, vgg, vgg, xn, mod, bg, w_ret, w_dif, w_o, g1, b1)


HALO = 16
FF_CHUNK = 256


def _ffn_down_kernel(u_ref, up_ref, un_ref, gt_ref, cw_ref, cb_ref, wd_ref, x1_ref, mod_ref, g2_ref, b2_ref, o_ref,
                     *, tm, n_tiles):
    i = pl.program_id(1)
    row = lax.broadcasted_iota(jnp.int32, (tm, FF_CHUNK), 0)
    acc = jnp.zeros((tm, D_MODEL), F32)
    for j in range(D_FF // FF_CHUNK):
        sl = slice(j * FF_CHUNK, (j + 1) * FF_CHUNK)
        u = u_ref[0, :, sl].astype(F32)
        prev_row = jnp.where(i > 0, up_ref[0, :, sl].astype(F32)[HALO - 1:HALO], 0.0)
        next_row = jnp.where(i < n_tiles - 1, un_ref[0, :, sl].astype(F32)[0:1], 0.0)
        u_m1 = jnp.where(row == 0, prev_row, pltpu.roll(u, 1, axis=0))
        u_p1 = jnp.where(row == tm - 1, next_row, pltpu.roll(u, tm - 1, axis=0))
        cw = cw_ref[:, sl]
        t = cb_ref[:, sl] + u_m1 * cw[0:1] + u * cw[1:2] + u_p1 * cw[2:3]
        act = 0.5 * t * (1.0 + lax.erf(t * np.float32(np.sqrt(0.5)))) * gt_ref[0, :, sl].astype(F32)
        acc = acc + _dot(act.astype(BF16), wd_ref[sl, :])
    m = mod_ref[0]
    o_ref[0] = _layer_norm(ALPHA * x1_ref[0] + m[5:6] * acc, g2_ref[...], b2_ref[...])


def _ffn_down_call(ug, conv_w, conv_b, w_down, x1, mod, g2, b2, *, tm=512):
    bsz, s, d = x1.shape
    tm = min(tm, s)
    n_tiles = s // tm
    hb = tm // HALO
    n_halo = s // HALO
    const = lambda b, i: (0, 0)
    return pl.pallas_call(
        functools.partial(_ffn_down_kernel, tm=tm, n_tiles=n_tiles),
        out_shape=jax.ShapeDtypeStruct((bsz, s, d), F32),
        grid=(bsz, n_tiles),
        in_specs=[pl.BlockSpec((1, tm, D_FF), lambda b, i: (b, i, 0)),
                  pl.BlockSpec((1, HALO, D_FF), lambda b, i: (b, jnp.maximum(i * hb - 1, 0), 0)),
                  pl.BlockSpec((1, HALO, D_FF), lambda b, i: (b, jnp.minimum((i + 1) * hb, n_halo - 1), 0)),
                  pl.BlockSpec((1, tm, D_FF), lambda b, i: (b, i, 1)),
                  pl.BlockSpec((CONV_W, D_FF), const),
                  pl.BlockSpec((1, D_FF), const),
                  pl.BlockSpec((D_FF, d), const),
                  pl.BlockSpec((1, tm, d), lambda b, i: (b, i, 0)),
                  pl.BlockSpec((1, 6, d), lambda b, i: (b, 0, 0)),
                  pl.BlockSpec((1, d), const),
                  pl.BlockSpec((1, d), const)],
        out_specs=pl.BlockSpec((1, tm, d), lambda b, i: (b, i, 0)),
        compiler_params=_cparams(("arbitrary", "arbitrary")),
        name="ffn_down",
    )(ug, ug, ug, ug, conv_w, conv_b, w_down, x1, mod, g2, b2)


def _axial_angles(s, head_dim):
    rows = s // GRID_W
    row = jnp.repeat(jnp.arange(rows, dtype=F32), GRID_W)
    col = jnp.tile(jnp.arange(GRID_W, dtype=F32), rows)
    half = head_dim // 2
    inv = ROPE_BASE ** (-(jnp.arange(0, half, 2, dtype=F32) / half))
    return jnp.concatenate([row[:, None] * inv, col[:, None] * inv], axis=-1)


def _ret_tables(s):
    ang = _axial_angles(s, RET_DK)
    cos, sin = jnp.cos(ang), jnp.sin(ang)
    k_scale = RET_DK ** -0.5
    return jnp.stack([cos, cos * k_scale]), jnp.stack([sin, sin * k_scale])


def _dif_tables(s):
    ang = _axial_angles(s, DIFF_DH)
    cos, sin = jnp.cos(ang), jnp.sin(ang)
    cos_t = jnp.tile(cos, (1, 4))
    sin_t = jnp.concatenate([-sin, -sin, sin, sin], axis=-1)
    q_scale = DIFF_DH ** -0.5
    return jnp.stack([cos_t * q_scale, cos_t]), jnp.stack([sin_t * q_scale, sin_t])


def _dif_col_perm():
    q4 = DIFF_DH // 2
    g = np.concatenate([np.arange(0, q4), np.arange(2 * q4, 3 * q4), np.arange(q4, 2 * q4), np.arange(3 * q4, 4 * q4)])
    return np.concatenate([h * LANES + g for h in range(2 * DIFF_HEADS)])


def kernel(x, c, ctx, c_ctx, ln_in_g, ln_in_b, w_mod, b_mod, w_in, b_gate, ret_decay_logit, diff_lambda, diff_subln_g,
           w_ret_out, w_diff_out, w_o, ln1_g, ln1_b, w_up, conv_w, conv_b, w_down, ln2_g, ln2_b):
    bsz, s, d = x.shape
    p_len = ctx.shape[1]
    row2 = lambda v: v.reshape(1, -1)

    n_rows = ((bsz + 1 + 7) // 8) * 8
    cc = jnp.zeros((n_rows, d), F32).at[:bsz].set(c).at[bsz].set(c_ctx)
    mod = _mod_call(cc, w_mod[0], row2(b_mod[0])).reshape(n_rows, 6, d)

    xn, h1 = _ln_mod_call(x, row2(ln_in_g), row2(ln_in_b), mod, mod_row=None, emit_xn=True, ts=512)
    (hc,) = _ln_mod_call(ctx, row2(ln_in_g), row2(ln_in_b), mod, mod_row=bsz, emit_xn=False, ts=256)

    w = w_in[0]
    w_qkd = w[:, OFF_QD:OFF_VD][:, _dif_col_perm()]
    w_bf = jnp.concatenate([w[:, :OFF_QD], w_qkd, w[:, OFF_VD:]], axis=1).astype(BF16)

    h1f = h1.reshape(bsz * s, d)
    hcf = hc.reshape(bsz * p_len, d)
    qk_r = _proj_call(h1f, w_bf, col0=OFF_QR, ncols=OFF_VR - OFF_QR, out_dtype=BF16, epi="rope_ret",
                      tables=_ret_tables(s), seq=s, name="proj_qk_ret").reshape(bsz, s, -1)
    vg_r = _proj_call(h1f, w_bf, col0=OFF_VR, ncols=OFF_QD - OFF_VR, out_dtype=BF16,
                      name="proj_vg_ret").reshape(bsz, s, -1)
    qk_d = _proj_call(h1f, w_bf, col0=OFF_QD, ncols=OFF_VD - OFF_QD, out_dtype=BF16, epi="rope_dif",
                      tables=_dif_tables(s), seq=s, name="proj_qk_dif").reshape(bsz, s, -1)
    vgg = _proj_call(h1f, w_bf, col0=OFF_VD, ncols=N_IN - OFF_VD, out_dtype=BF16,
                     name="proj_v_gates").reshape(bsz, s, -1)
    kr_c = _proj_call(hcf, w_bf, col0=OFF_KR, ncols=D_MODEL, out_dtype=F32, scale=RET_DK ** -0.5,
                      name="proj_ctx_kr").reshape(bsz, p_len, -1)
    vr_c = _proj_call(hcf, w_bf, col0=OFF_VR, ncols=RET_HEADS * RET_DV, out_dtype=BF16,
                      name="proj_ctx_vr").reshape(bsz, p_len, -1)
    kd_c = _proj_call(hcf, w_bf, col0=OFF_KD, ncols=D_MODEL, out_dtype=BF16,
                      name="proj_ctx_kd").reshape(bsz, p_len, -1)
    vd_c = _proj_call(hcf, w_bf, col0=OFF_VD, ncols=D_MODEL, out_dtype=BF16,
                      name="proj_ctx_vd").reshape(bsz, p_len, -1)

    log_g = jax.nn.log_sigmoid(ret_decay_logit[0].astype(F32))
    lg_tab = jnp.zeros((RET_HEADS, 8, LANES), F32).at[:, :2, :].set(
        jnp.broadcast_to(log_g.T[:, :, None], (RET_HEADS, 2, LANES)))
    z_ret = _ret_call(lg_tab, qk_r, vg_r, kr_c, vr_c)
    z_dif = _diff_call(diff_lambda[0].astype(F32), row2(diff_subln_g[0]), qk_d, vgg, kd_c, vd_c)

    x1, h2 = _merge_call(z_ret, z_dif, vgg, xn, mod, b_gate[0].reshape(2, d), w_ret_out[0].astype(BF16),
                         w_diff_out[0].astype(BF16), w_o[0].astype(BF16), row2(ln1_g[0]), row2(ln1_b[0]))

    ug = _proj_call(h2.reshape(bsz * s, d), w_up[0].astype(BF16), col0=0, ncols=2 * D_FF, out_dtype=BF16,
                    tm=512, tn=D_FF, name="ffn_up").reshape(bsz, s, -1)
    return _ffn_down_call(ug, conv_w[0], row2(conv_b[0]), w_down[0].astype(BF16), x1, mod, row2(ln2_g[0]),
                          row2(ln2_b[0]))
```

```python
import functools
import math

import numpy as np
import jax
import jax.numpy as jnp
from jax import lax
from jax.experimental import pallas as pl
from jax.experimental.pallas import tpu as pltpu

F32 = jnp.float32
BF16 = jnp.bfloat16

D_MODEL = 1024
GRID_W = 64
RET_HEADS = 4
RET_DK = D_MODEL // RET_HEADS
RET_DV = 2 * RET_DK
DIFF_DH = 64
DIFF_HEADS = D_MODEL // (2 * DIFF_DH)
D_FF = ((8 * D_MODEL // 3 + 127) // 128) * 128
CONV_W = 3
ROPE_BASE = 10000.0
LN_EPS = 1e-5
DEPTH = 1
ALPHA = (2.0 * DEPTH) ** 0.25
LAM_INIT = 0.8 - 0.6 * math.exp(-0.3 * 0)

OFF_QR, OFF_KR, OFF_VR, OFF_GR = 0, RET_HEADS * RET_DK, 2 * RET_HEADS * RET_DK, 2 * RET_HEADS * RET_DK + RET_HEADS * RET_DV
OFF_QD = OFF_GR + RET_HEADS * RET_DV
OFF_KD = OFF_QD + D_MODEL
OFF_VD = OFF_KD + D_MODEL
OFF_GATE_R = OFF_VD + D_MODEL
OFF_GATE_D = OFF_GATE_R + D_MODEL
N_IN = OFF_GATE_D + D_MODEL

LANES = 128
RET_CHUNK = 256
VMEM_LIMIT = 60 * 1024 * 1024


def _cparams(sem):
    return pltpu.CompilerParams(dimension_semantics=sem, vmem_limit_bytes=VMEM_LIMIT)


def _layer_norm(x, g, b):
    mu = jnp.mean(x, axis=-1, keepdims=True)
    xc = x - mu
    var = jnp.mean(xc * xc, axis=-1, keepdims=True)
    return xc * lax.rsqrt(var + LN_EPS) * g + b


def _dot(a, b):
    return jnp.dot(a, b, preferred_element_type=F32)


def _dot_nt(a, b):
    return lax.dot_general(a, b, (((1,), (1,)), ((), ())), preferred_element_type=F32)


def _dot_tn(a, b):
    return lax.dot_general(a, b, (((0,), (0,)), ((), ())), preferred_element_type=F32)


def _mod_kernel(c_ref, w_ref, b_ref, o_ref):
    cond = jax.nn.silu(c_ref[...])
    o_ref[...] = _dot(cond.astype(BF16), w_ref[...].astype(BF16)) + b_ref[...]


def _mod_call(cc, w_mod, b_mod):
    rows, d = cc.shape
    n = w_mod.shape[1]
    tn = 1024
    return pl.pallas_call(
        _mod_kernel,
        out_shape=jax.ShapeDtypeStruct((rows, n), F32),
        grid=(n // tn,),
        in_specs=[pl.BlockSpec((rows, d), lambda j: (0, 0)),
                  pl.BlockSpec((d, tn), lambda j: (0, j)),
                  pl.BlockSpec((1, tn), lambda j: (0, j))],
        out_specs=pl.BlockSpec((rows, tn), lambda j: (0, j)),
        compiler_params=_cparams(("arbitrary",)),
        name="mod",
    )(cc, w_mod, b_mod)


def _ln_mod_kernel(x_ref, g_ref, b_ref, mod_ref, *out_refs, emit_xn):
    xn = _layer_norm(x_ref[0], g_ref[...], b_ref[...])
    m = mod_ref[0]
    h = (xn * (1.0 + m[1:2]) + m[0:1]).astype(BF16)
    if emit_xn:
        out_refs[0][0] = xn
        out_refs[1][0] = h
    else:
        out_refs[0][0] = h


def _ln_mod_call(x, g, b, mod, *, mod_row, emit_xn, ts):
    bsz, s, d = x.shape
    ts = min(ts, s)
    mod_map = (lambda i, j: (i, 0, 0)) if mod_row is None else (lambda i, j: (mod_row, 0, 0))
    out_shape = [jax.ShapeDtypeStruct((bsz, s, d), BF16)]
    out_specs = [pl.BlockSpec((1, ts, d), lambda i, j: (i, j, 0))]
    if emit_xn:
        out_shape = [jax.ShapeDtypeStruct((bsz, s, d), F32)] + out_shape
        out_specs = [pl.BlockSpec((1, ts, d), lambda i, j: (i, j, 0))] + out_specs
    return pl.pallas_call(
        functools.partial(_ln_mod_kernel, emit_xn=emit_xn),
        out_shape=out_shape,
        grid=(bsz, s // ts),
        in_specs=[pl.BlockSpec((1, ts, d), lambda i, j: (i, j, 0)),
                  pl.BlockSpec((1, d), lambda i, j: (0, 0)),
                  pl.BlockSpec((1, d), lambda i, j: (0, 0)),
                  pl.BlockSpec((1, 6, d), mod_map)],
        out_specs=out_specs,
        compiler_params=_cparams(("arbitrary", "arbitrary")),
        name="ln_mod_x" if emit_xn else "ln_mod_ctx",
    )(x, g, b, mod)


def _proj_kernel(x_ref, w_ref, *rest, epi, scale, tn):
    acc = _dot(x_ref[...], w_ref[...])
    if epi == "plain":
        (o_ref,) = rest
        if scale != 1.0:
            acc = acc * scale
        o_ref[...] = acc.astype(o_ref.dtype)
    elif epi == "silu":
        (o_ref,) = rest
        o_ref[...] = jax.nn.silu(acc).astype(o_ref.dtype)
    elif epi == "rope_ret":
        cos_ref, sin_ref, o_ref = rest
        cos, sin = cos_ref[0], sin_ref[0]
        for h in range(tn // (2 * LANES)):
            a = h * 2 * LANES
            x1 = acc[:, a:a + LANES]
            x2 = acc[:, a + LANES:a + 2 * LANES]
            o_ref[:, a:a + LANES] = (x1 * cos - x2 * sin).astype(o_ref.dtype)
            o_ref[:, a + LANES:a + 2 * LANES] = (x2 * cos + x1 * sin).astype(o_ref.dtype)
    else:
        cos_ref, sin_ref, o_ref = rest
        cos, sin = cos_ref[0], sin_ref[0]
        for g in range(tn // LANES):
            a = g * LANES
            xg = acc[:, a:a + LANES]
            o_ref[:, a:a + LANES] = (xg * cos + pltpu.roll(xg, LANES // 2, axis=1) * sin).astype(o_ref.dtype)


def _proj_call(x, w, *, col0, ncols, out_dtype, epi="plain", scale=1.0, tables=None, seq=None, tm=1024, tn=1024, name="proj"):
    m, k = x.shape
    tm = min(tm, m if seq is None else seq)
    tn = min(tn, ncols)
    assert m % tm == 0 and ncols % tn == 0 and col0 % tn == 0
    jb = col0 // tn
    in_specs = [pl.BlockSpec((tm, k), lambda j, i: (i, 0)),
                pl.BlockSpec((k, tn), lambda j, i: (0, jb + j))]
    args = [x, w]
    if tables is not None:
        assert seq % tm == 0
        ns = seq // tm
        tspec = pl.BlockSpec((1, tm, LANES), lambda j, i: (j, i % ns, 0))
        in_specs += [tspec, tspec]
        args += list(tables)
    return pl.pallas_call(
        functools.partial(_proj_kernel, epi=epi, scale=scale, tn=tn),
        out_shape=jax.ShapeDtypeStruct((m, ncols), out_dtype),
        grid=(ncols // tn, m // tm),
        in_specs=in_specs,
        out_specs=pl.BlockSpec((tm, tn), lambda j, i: (i, j)),
        compiler_params=_cparams(("arbitrary", "arbitrary")),
        name=name,
    )(*args)


def _ret_kernel(lg_ref, q_ref, k_ref, v_ref, g_ref, kc_ref, vc_ref, o_ref, y_scr, sf_scr, sb_scr, *, chunk, n_chunks):
    c_len = chunk
    lg = lg_ref[0]
    lgf = lg[0:1, :]
    lgb = lg[1:2, :]

    def col(x):
        return x[:, 0:1]

    p_len = kc_ref.shape[1]
    pos_p = lax.broadcasted_iota(jnp.int32, (p_len, LANES), 0).astype(F32)
    kc = kc_ref[0]
    vc = vc_ref[0]
    sf_scr[...] = _dot_tn((kc * col(jnp.exp(((p_len - 1.0) - pos_p) * lgf))).astype(BF16), vc)
    sb_scr[...] = _dot_tn((kc * col(jnp.exp(pos_p * lgb))).astype(BF16), vc)

    pos = lax.broadcasted_iota(jnp.int32, (c_len, LANES), 0).astype(F32)
    dq_f = col(jnp.exp((pos + 1.0) * lgf))
    dk_f = col(jnp.exp((c_len - 1.0 - pos) * lgf))
    dq_b = col(jnp.exp((c_len - pos) * lgb))
    dk_b = col(jnp.exp(pos * lgb))
    ds_f = col(jnp.exp(c_len * lgf))
    ds_b = col(jnp.exp(c_len * lgb))
    rel = (lax.broadcasted_iota(jnp.int32, (c_len, c_len), 0)
           - lax.broadcasted_iota(jnp.int32, (c_len, c_len), 1)).astype(F32)
    d_in = (jnp.where(rel >= 0, jnp.exp(jnp.maximum(rel, 0.0) * col(lgf)), 0.0)
            + jnp.where(rel <= 0, jnp.exp(jnp.maximum(-rel, 0.0) * col(lgb)), 0.0))

    def rows(c):
        return pl.ds(pl.multiple_of(c * c_len, c_len), c_len)

    def fwd_part(c):
        q, k, v = q_ref[0, rows(c), :], k_ref[0, rows(c), :], v_ref[0, rows(c), :]
        inner = _dot_nt(q, k) * d_in
        s = sf_scr[...]
        y = _dot(inner.astype(BF16), v) + _dot(q, s.astype(BF16)) * dq_f
        sf_scr[...] = s * ds_f + _dot_tn((k.astype(F32) * dk_f).astype(BF16), v)
        return y

    def bwd_part(c):
        q, k, v = q_ref[0, rows(c), :], k_ref[0, rows(c), :], v_ref[0, rows(c), :]
        s = sb_scr[...]
        y = _dot(q, s.astype(BF16)) * dq_b
        sb_scr[...] = s * ds_b + _dot_tn((k.astype(F32) * dk_b).astype(BF16), v)
        return y

    def finish(c, y):
        mu = jnp.mean(y, axis=-1, keepdims=True)
        yc = y - mu
        var = jnp.mean(yc * yc, axis=-1, keepdims=True)
        gate = g_ref[0, rows(c), :].astype(F32)
        o_ref[0, rows(c), :] = (gate * (yc * lax.rsqrt(var + LN_EPS))).astype(o_ref.dtype)

    def first_half(i, carry):
        y_scr[rows(i), :] = fwd_part(i)
        y_scr[rows(n_chunks - 1 - i), :] = bwd_part(n_chunks - 1 - i)
        return carry

    def second_half(i, carry):
        cb = n_chunks - 1 - i
        finish(i, y_scr[rows(i), :] + fwd_part(i))
        finish(cb, y_scr[rows(cb), :] + bwd_part(cb))
        return carry

    lax.fori_loop(0, n_chunks // 2, first_half, 0, unroll=2)
    lax.fori_loop(n_chunks // 2, n_chunks, second_half, 0, unroll=2)


def _ret_call(lg_tab, qk, v, sg, krc, vrc):
    bsz, s, _ = qk.shape
    p_len = krc.shape[1]
    chunk = min(RET_CHUNK, s)
    assert s % (2 * chunk) == 0
    return pl.pallas_call(
        functools.partial(_ret_kernel, chunk=chunk, n_chunks=s // chunk),
        out_shape=jax.ShapeDtypeStruct((bsz, s, RET_HEADS * RET_DV), BF16),
        grid=(bsz, RET_HEADS),
        in_specs=[pl.BlockSpec((1, 8, LANES), lambda b, h: (h, 0, 0)),
                  pl.BlockSpec((1, s, RET_DK), lambda b, h: (b, 0, h)),
                  pl.BlockSpec((1, s, RET_DK), lambda b, h: (b, 0, RET_HEADS + h)),
                  pl.BlockSpec((1, s, RET_DV), lambda b, h: (b, 0, h)),
                  pl.BlockSpec((1, s, RET_DV), lambda b, h: (b, 0, h)),
                  pl.BlockSpec((1, p_len, RET_DK), lambda b, h: (b, 0, h)),
                  pl.BlockSpec((1, p_len, RET_DV), lambda b, h: (b, 0, h))],
        out_specs=pl.BlockSpec((1, s, RET_DV), lambda b, h: (b, 0, h)),
        scratch_shapes=[pltpu.VMEM((s, RET_DV), F32),
                        pltpu.VMEM((RET_DK, RET_DV), F32),
                        pltpu.VMEM((RET_DK, RET_DV), F32)],
        compiler_params=_cparams(("arbitrary", "arbitrary")),
        name="retention",
    )(lg_tab, qk, qk, v, sg, krc, vrc)


DIFF_TQI = 128
DIFF_TK = 256
ONES_ROWS = 16
NEG_BIG = -1e30


def _diff_kernel(dl_ref, gsub_ref, q_ref, k_ref, v_ref, kc_ref, vc_ref, o_ref, vt_scr, sa_scr, sb_scr, q2_scr,
                 *, s_len, p_len):
    tk = DIFF_TK
    n_lat, n_ctx = s_len // tk, p_len // tk
    n_blk = n_lat + n_ctx

    tqi = DIFF_TQI
    hd = 2 * DIFF_DH
    n_sub = s_len // tqi

    ones = jnp.ones((ONES_ROWS, tk), BF16)
    for j in range(n_blk):
        src = v_ref[0, j * tk:(j + 1) * tk, :] if j < n_lat else vc_ref[0, (j - n_lat) * tk:(j - n_lat + 1) * tk, :]
        vt_scr[j, 0:hd, :] = src.astype(F32).T.astype(BF16)
        vt_scr[j, hd:, :] = ones

    feat = lax.broadcasted_iota(jnp.int32, (hd, tqi), 0)
    comp0 = (feat & (hd // 4)) == 0

    def prep_q(t, carry):
        qt = q_ref[0, pl.ds(pl.multiple_of(t * tqi, tqi), tqi), :].astype(F32).T
        q2_scr[t] = jnp.concatenate([jnp.where(comp0, qt, 0.0), jnp.where(comp0, 0.0, qt)], axis=1).astype(BF16)
        return carry

    lax.fori_loop(0, n_sub, prep_q, 0, unroll=4)

    dl = dl_ref[...]
    lam = (jnp.exp(jnp.sum(dl[0:1] * dl[1:2], axis=-1, keepdims=True))
           - jnp.exp(jnp.sum(dl[2:3] * dl[3:4], axis=-1, keepdims=True)) + LAM_INIT)
    gsub = gsub_ref[...] * (1.0 - LAM_INIT)

    def key_block(j):
        return k_ref[0, j * tk:(j + 1) * tk, :] if j < n_lat else kc_ref[0, (j - n_lat) * tk:(j - n_lat + 1) * tk, :]

    def stage(t_val, s_val, mx, t_score, s_score):
        mx_next = jnp.full((8, 2 * tqi), NEG_BIG, F32)
        if t_val is not None:
            m = jnp.max(mx, axis=0, keepdims=True)
            acc = jnp.zeros((hd + ONES_ROWS, 2 * tqi), F32)
        if t_score is not None:
            q2 = q2_scr[t_score]
        for j in range(n_blk):
            if t_val is not None:
                p = jnp.exp(s_val[j] - m)
                acc = acc + _dot(vt_scr[j], p.astype(BF16))
            if t_score is not None:
                s = _dot(key_block(j), q2)
                s_score[j] = s
                mx_next = jnp.maximum(mx_next, jnp.max(s.reshape(tk // 8, 8, 2 * tqi), axis=0))
        if t_val is not None:
            o2 = acc[0:hd] * (1.0 / acc[hd:hd + 1])
            o = (o2[:, :tqi] - lam * o2[:, tqi:]).T
            o = o * lax.rsqrt(jnp.mean(o * o, axis=-1, keepdims=True) + LN_EPS) * gsub
            row0 = t_val * tqi if isinstance(t_val, int) else pl.multiple_of(t_val * tqi, tqi)
            o_ref[0, pl.ds(row0, tqi), :] = o.astype(o_ref.dtype)
        return mx_next

    mx = stage(None, None, None, 0, sa_scr)

    def stage_pair(u, mx):
        mx = stage(2 * u, sa_scr, mx, 2 * u + 1, sb_scr)
        return stage(2 * u + 1, sb_scr, mx, 2 * u + 2, sa_scr)

    mx = lax.fori_loop(0, n_sub // 2 - 1, stage_pair, mx)
    mx = stage(n_sub - 2, sa_scr, mx, n_sub - 1, sb_scr)
    stage(n_sub - 1, sb_scr, mx, None, None)


def _diff_call(dl, gsub, qk, vgg, kdc, vdc):
    bsz, s, _ = qk.shape
    p_len = kdc.shape[1]
    hd = 2 * DIFF_DH
    n_blk = (s + p_len) // DIFF_TK
    assert s % DIFF_TK == 0 and p_len % DIFF_TK == 0 and s % (2 * DIFF_TQI) == 0
    return pl.pallas_call(
        functools.partial(_diff_kernel, s_len=s, p_len=p_len),
        out_shape=jax.ShapeDtypeStruct((bsz, s, D_MODEL), BF16),
        scratch_shapes=[pltpu.VMEM((n_blk, hd + ONES_ROWS, DIFF_TK), BF16),
                        pltpu.VMEM((n_blk, DIFF_TK, 2 * DIFF_TQI), F32),
                        pltpu.VMEM((n_blk, DIFF_TK, 2 * DIFF_TQI), F32),
                        pltpu.VMEM((s // DIFF_TQI, hd, 2 * DIFF_TQI), BF16)],
        grid=(bsz, DIFF_HEADS),
        in_specs=[pl.BlockSpec((4, DIFF_DH), lambda b, h: (0, 0)),
                  pl.BlockSpec((1, hd), lambda b, h: (0, 0)),
                  pl.BlockSpec((1, s, hd), lambda b, h: (b, 0, h)),
                  pl.BlockSpec((1, s, hd), lambda b, h: (b, 0, DIFF_HEADS + h)),
                  pl.BlockSpec((1, s, hd), lambda b, h: (b, 0, h)),
                  pl.BlockSpec((1, p_len, hd), lambda b, h: (b, 0, h)),
                  pl.BlockSpec((1, p_len, hd), lambda b, h: (b, 0, h))],
        out_specs=pl.BlockSpec((1, s, hd), lambda b, h: (b, 0, h)),
        compiler_params=_cparams(("arbitrary", "arbitrary")),
        name="diff_attn",
    )(dl, gsub, qk, qk, vgg, kdc, vdc)


def _merge_kernel(zr_ref, zd_ref, gr_ref, gd_ref, xn_ref, mod_ref, bg_ref, wr_ref, wd_ref, wo_ref, g1_ref, b1_ref,
                  x1_ref, h2_ref):
    y_ret = _dot(zr_ref[0], wr_ref[...])
    y_dif = _dot(zd_ref[0], wd_ref[...])
    bg = bg_ref[...]
    gate_r = jax.nn.sigmoid(gr_ref[0].astype(F32) + bg[0:1])
    gate_d = jax.nn.sigmoid(gd_ref[0].astype(F32) + bg[1:2])
    y_mix = _dot((gate_r * y_ret + gate_d * y_dif).astype(BF16), wo_ref[...])
    m = mod_ref[0]
    x1 = _layer_norm(ALPHA * xn_ref[0] + m[2:3] * y_mix, g1_ref[...], b1_ref[...])
    x1_ref[0] = x1
    h2_ref[0] = (x1 * (1.0 + m[4:5]) + m[3:4]).astype(BF16)


def _merge_call(z_ret, z_dif, vgg, xn, mod, bg, w_ret, w_dif, w_o, g1, b1, *, tm=512):
    bsz, s, d = xn.shape
    tm = min(tm, s)
    const = lambda b, i: (0, 0)
    return pl.pallas_call(
        _merge_kernel,
        out_shape=[jax.ShapeDtypeStruct((bsz, s, d), F32), jax.ShapeDtypeStruct((bsz, s, d), BF16)],
        grid=(bsz, s // tm),
        in_specs=[pl.BlockSpec((1, tm, RET_HEADS * RET_DV), lambda b, i: (b, i, 0)),
                  pl.BlockSpec((1, tm, d), lambda b, i: (b, i, 0)),
                  pl.BlockSpec((1, tm, d), lambda b, i: (b, i, 1)),
                  pl.BlockSpec((1, tm, d), lambda b, i: (b, i, 2)),
                  pl.BlockSpec((1, tm, d), lambda b, i: (b, i, 0)),
                  pl.BlockSpec((1, 6, d), lambda b, i: (b, 0, 0)),
                  pl.BlockSpec((2, d), const),
                  pl.BlockSpec(w_ret.shape, const),
                  pl.BlockSpec(w_dif.shape, const),
                  pl.BlockSpec(w_o.shape, const),
                  pl.BlockSpec((1, d), const),
                  pl.BlockSpec((1, d), const)],
        out_specs=[pl.BlockSpec((1, tm, d), lambda b, i: (b, i, 0)),
                   pl.BlockSpec((1, tm, d), lambda b, i: (b, i, 0))],
        compiler_params=_cparams(("arbitrary", "arbitrary")),
        name="merge",
    )(z_ret, z_dif, vgg, vgg, xn, mod, bg, w_ret, w_dif, w_o, g1, b1)


HALO = 16
FF_CHUNK = 256


def _ffn_kernel(h_ref, hp_ref, hn_ref, x1_ref, mod_ref, wu_ref, cw_ref, cb_ref, wd_ref, g2_ref, b2_ref, o_ref,
                *, tm, n_tiles):
    i = pl.program_id(1)
    h = h_ref[0]
    h_prev = jnp.where(i > 0, hp_ref[0], jnp.zeros_like(hp_ref[0]))
    h_next = jnp.where(i < n_tiles - 1, hn_ref[0], jnp.zeros_like(hn_ref[0]))
    h_ext = jnp.concatenate([h_prev, h, h_next], axis=0)
    rows = tm + 2 * HALO
    acc = jnp.zeros((tm, D_MODEL), F32)
    for j in range(D_FF // FF_CHUNK):
        sl = slice(j * FF_CHUNK, (j + 1) * FF_CHUNK)
        u = _dot(h_ext, wu_ref[:, sl])
        gate = _dot(h, wu_ref[:, D_FF + j * FF_CHUNK:D_FF + (j + 1) * FF_CHUNK])
        cw = cw_ref[:, sl]
        t = (cb_ref[:, sl] + pltpu.roll(u, 1, axis=0)[HALO:HALO + tm] * cw[0:1] + u[HALO:HALO + tm] * cw[1:2]
             + pltpu.roll(u, rows - 1, axis=0)[HALO:HALO + tm] * cw[2:3])
        act = 0.5 * t * (1.0 + lax.erf(t * np.float32(np.sqrt(0.5)))) * gate
        acc = acc + _dot(act.astype(BF16), wd_ref[sl, :])
    m = mod_ref[0]
    o_ref[0] = _layer_norm(ALPHA * x1_ref[0] + m[5:6] * acc, g2_ref[...], b2_ref[...])


def _ffn_call(h2, x1, mod, w_up, conv_w, conv_b, w_down, g2, b2, *, tm=1024):
    bsz, s, d = x1.shape
    tm = min(tm, s)
    n_tiles = s // tm
    hb = tm // HALO
    n_halo = s // HALO
    const = lambda b, i: (0, 0)
    return pl.pallas_call(
        functools.partial(_ffn_kernel, tm=tm, n_tiles=n_tiles),
        out_shape=jax.ShapeDtypeStruct((bsz, s, d), F32),
        grid=(bsz, n_tiles),
        in_specs=[pl.BlockSpec((1, tm, d), lambda b, i: (b, i, 0)),
                  pl.BlockSpec((1, HALO, d), lambda b, i: (b, jnp.maximum(i * hb - 1, 0), 0)),
                  pl.BlockSpec((1, HALO, d), lambda b, i: (b, jnp.minimum((i + 1) * hb, n_halo - 1), 0)),
                  pl.BlockSpec((1, tm, d), lambda b, i: (b, i, 0)),
                  pl.BlockSpec((1, 6, d), lambda b, i: (b, 0, 0)),
                  pl.BlockSpec((d, 2 * D_FF), const, pipeline_mode=pl.Buffered(1)),
                  pl.BlockSpec((CONV_W, D_FF), const),
                  pl.BlockSpec((1, D_FF), const),
                  pl.BlockSpec((D_FF, d), const, pipeline_mode=pl.Buffered(1)),
                  pl.BlockSpec((1, d), const),
                  pl.BlockSpec((1, d), const)],
        out_specs=pl.BlockSpec((1, tm, d), lambda b, i: (b, i, 0)),
        compiler_params=_cparams(("arbitrary", "arbitrary")),
        name="ffn",
    )(h2, h2, h2, x1, mod, w_up, conv_w, conv_b, w_down, g2, b2)


def _axial_angles(s, head_dim):
    rows = s // GRID_W
    row = jnp.repeat(jnp.arange(rows, dtype=F32), GRID_W)
    col = jnp.tile(jnp.arange(GRID_W, dtype=F32), rows)
    half = head_dim // 2
    inv = ROPE_BASE ** (-(jnp.arange(0, half, 2, dtype=F32) / half))
    return jnp.concatenate([row[:, None] * inv, col[:, None] * inv], axis=-1)


def _ret_tables(s):
    ang = _axial_angles(s, RET_DK)
    cos, sin = jnp.cos(ang), jnp.sin(ang)
    k_scale = RET_DK ** -0.5
    return jnp.stack([cos, cos * k_scale]), jnp.stack([sin, sin * k_scale])


def _dif_tables(s):
    ang = _axial_angles(s, DIFF_DH)
    cos, sin = jnp.cos(ang), jnp.sin(ang)
    cos_t = jnp.tile(cos, (1, 4))
    sin_t = jnp.concatenate([-sin, -sin, sin, sin], axis=-1)
    q_scale = DIFF_DH ** -0.5
    return jnp.stack([cos_t * q_scale, cos_t]), jnp.stack([sin_t * q_scale, sin_t])


def _dif_col_perm():
    q4 = DIFF_DH // 2
    g = np.concatenate([np.arange(0, q4), np.arange(2 * q4, 3 * q4), np.arange(q4, 2 * q4), np.arange(3 * q4, 4 * q4)])
    return np.concatenate([h * LANES + g for h in range(2 * DIFF_HEADS)])


def kernel(x, c, ctx, c_ctx, ln_in_g, ln_in_b, w_mod, b_mod, w_in, b_gate, ret_decay_logit, diff_lambda, diff_subln_g,
           w_ret_out, w_diff_out, w_o, ln1_g, ln1_b, w_up, conv_w, conv_b, w_down, ln2_g, ln2_b):
    bsz, s, d = x.shape
    p_len = ctx.shape[1]
    row2 = lambda v: v.reshape(1, -1)

    n_rows = ((bsz + 1 + 7) // 8) * 8
    cc = jnp.zeros((n_rows, d), F32).at[:bsz].set(c).at[bsz].set(c_ctx)
    mod = _mod_call(cc, w_mod[0], row2(b_mod[0])).reshape(n_rows, 6, d)

    xn, h1 = _ln_mod_call(x, row2(ln_in_g), row2(ln_in_b), mod, mod_row=None, emit_xn=True, ts=512)
    (hc,) = _ln_mod_call(ctx, row2(ln_in_g), row2(ln_in_b), mod, mod_row=bsz, emit_xn=False, ts=256)

    w = w_in[0]
    w_qkd = w[:, OFF_QD:OFF_VD][:, _dif_col_perm()]
    w_bf = jnp.concatenate([w[:, :OFF_QD], w_qkd, w[:, OFF_VD:]], axis=1).astype(BF16)

    h1f = h1.reshape(bsz * s, d)
    hcf = hc.reshape(bsz * p_len, d)
    qk_r = _proj_call(h1f, w_bf, col0=OFF_QR, ncols=OFF_VR - OFF_QR, out_dtype=BF16, epi="rope_ret",
                      tables=_ret_tables(s), seq=s, name="proj_qk_ret").reshape(bsz, s, -1)
    v_r = _proj_call(h1f, w_bf, col0=OFF_VR, ncols=OFF_GR - OFF_VR, out_dtype=BF16,
                     name="proj_v_ret").reshape(bsz, s, -1)
    sg_r = _proj_call(h1f, w_bf, col0=OFF_GR, ncols=OFF_QD - OFF_GR, out_dtype=BF16, epi="silu",
                      name="proj_g_ret").reshape(bsz, s, -1)
    qk_d = _proj_call(h1f, w_bf, col0=OFF_QD, ncols=OFF_VD - OFF_QD, out_dtype=BF16, epi="rope_dif",
                      tables=_dif_tables(s), seq=s, name="proj_qk_dif").reshape(bsz, s, -1)
    vgg = _proj_call(h1f, w_bf, col0=OFF_VD, ncols=N_IN - OFF_VD, out_dtype=BF16,
                     name="proj_v_gates").reshape(bsz, s, -1)
    kr_c = _proj_call(hcf, w_bf, col0=OFF_KR, ncols=D_MODEL, out_dtype=F32, scale=RET_DK ** -0.5,
                      name="proj_ctx_kr").reshape(bsz, p_len, -1)
    vr_c = _proj_call(hcf, w_bf, col0=OFF_VR, ncols=RET_HEADS * RET_DV, out_dtype=BF16,
                      name="proj_ctx_vr").reshape(bsz, p_len, -1)
    kd_c = _proj_call(hcf, w_bf, col0=OFF_KD, ncols=D_MODEL, out_dtype=BF16,
                      name="proj_ctx_kd").reshape(bsz, p_len, -1)
    vd_c = _proj_call(hcf, w_bf, col0=OFF_VD, ncols=D_MODEL, out_dtype=BF16,
                      name="proj_ctx_vd").reshape(bsz, p_len, -1)

    log_g = jax.nn.log_sigmoid(ret_decay_logit[0].astype(F32))
    lg_tab = jnp.zeros((RET_HEADS, 8, LANES), F32).at[:, :2, :].set(
        jnp.broadcast_to(log_g.T[:, :, None], (RET_HEADS, 2, LANES)))
    z_ret = _ret_call(lg_tab, qk_r, v_r, sg_r, kr_c, vr_c)
    z_dif = _diff_call(diff_lambda[0].astype(F32), row2(diff_subln_g[0]), qk_d, vgg, kd_c, vd_c)

    x1, h2 = _merge_call(z_ret, z_dif, vgg, xn, mod, b_gate[0].reshape(2, d), w_ret_out[0].astype(BF16),
                         w_diff_out[0].astype(BF16), w_o[0].astype(BF16), row2(ln1_g[0]), row2(ln1_b[0]))

    return _ffn_call(h2, x1, mod, w_up[0].astype(BF16), conv_w[0], row2(conv_b[0]), w_down[0].astype(BF16),
                     row2(ln2_g[0]), row2(ln2_b[0]))
```

```python
import functools
import math

import numpy as np
import jax
import jax.numpy as jnp
from jax import lax
from jax.experimental import pallas as pl
from jax.experimental.pallas import tpu as pltpu

F32 = jnp.float32
BF16 = jnp.bfloat16

D_MODEL = 1024
GRID_W = 64
RET_HEADS = 4
RET_DK = D_MODEL // RET_HEADS
RET_DV = 2 * RET_DK
DIFF_DH = 64
DIFF_HEADS = D_MODEL // (2 * DIFF_DH)
D_FF = ((8 * D_MODEL // 3 + 127) // 128) * 128
CONV_W = 3
ROPE_BASE = 10000.0
LN_EPS = 1e-5
DEPTH = 1
ALPHA = (2.0 * DEPTH) ** 0.25
LAM_INIT = 0.8 - 0.6 * math.exp(-0.3 * 0)

OFF_QR, OFF_KR, OFF_VR, OFF_GR = 0, RET_HEADS * RET_DK, 2 * RET_HEADS * RET_DK, 2 * RET_HEADS * RET_DK + RET_HEADS * RET_DV
OFF_QD = OFF_GR + RET_HEADS * RET_DV
OFF_KD = OFF_QD + D_MODEL
OFF_VD = OFF_KD + D_MODEL
OFF_GATE_R = OFF_VD + D_MODEL
OFF_GATE_D = OFF_GATE_R + D_MODEL
N_IN = OFF_GATE_D + D_MODEL

LANES = 128
RET_CHUNK = 256
VMEM_LIMIT = 60 * 1024 * 1024


def _cparams(sem):
    return pltpu.CompilerParams(dimension_semantics=sem, vmem_limit_bytes=VMEM_LIMIT)


def _layer_norm(x, g, b):
    mu = jnp.mean(x, axis=-1, keepdims=True)
    xc = x - mu
    var = jnp.mean(xc * xc, axis=-1, keepdims=True)
    return xc * lax.rsqrt(var + LN_EPS) * g + b


def _dot(a, b):
    return jnp.dot(a, b, preferred_element_type=F32)


def _dot_nt(a, b):
    return lax.dot_general(a, b, (((1,), (1,)), ((), ())), preferred_element_type=F32)


def _dot_tn(a, b):
    return lax.dot_general(a, b, (((0,), (0,)), ((), ())), preferred_element_type=F32)


def _mod_kernel(c_ref, w_ref, b_ref, o_ref):
    cond = jax.nn.silu(c_ref[...])
    o_ref[...] = _dot(cond.astype(BF16), w_ref[...].astype(BF16)) + b_ref[...]


def _mod_call(cc, w_mod, b_mod):
    rows, d = cc.shape
    n = w_mod.shape[1]
    tn = 1024
    return pl.pallas_call(
        _mod_kernel,
        out_shape=jax.ShapeDtypeStruct((rows, n), F32),
        grid=(n // tn,),
        in_specs=[pl.BlockSpec((rows, d), lambda j: (0, 0)),
                  pl.BlockSpec((d, tn), lambda j: (0, j)),
                  pl.BlockSpec((1, tn), lambda j: (0, j))],
        out_specs=pl.BlockSpec((rows, tn), lambda j: (0, j)),
        compiler_params=_cparams(("arbitrary",)),
        name="mod",
    )(cc, w_mod, b_mod)


def _ln_mod_kernel(x_ref, g_ref, b_ref, mod_ref, *out_refs, emit_xn):
    xn = _layer_norm(x_ref[0], g_ref[...], b_ref[...])
    m = mod_ref[0]
    h = (xn * (1.0 + m[1:2]) + m[0:1]).astype(BF16)
    if emit_xn:
        out_refs[0][0] = xn
        out_refs[1][0] = h
    else:
        out_refs[0][0] = h


def _ln_mod_call(x, g, b, mod, *, mod_row, emit_xn, ts):
    bsz, s, d = x.shape
    ts = min(ts, s)
    mod_map = (lambda i, j: (i, 0, 0)) if mod_row is None else (lambda i, j: (mod_row, 0, 0))
    out_shape = [jax.ShapeDtypeStruct((bsz, s, d), BF16)]
    out_specs = [pl.BlockSpec((1, ts, d), lambda i, j: (i, j, 0))]
    if emit_xn:
        out_shape = [jax.ShapeDtypeStruct((bsz, s, d), F32)] + out_shape
        out_specs = [pl.BlockSpec((1, ts, d), lambda i, j: (i, j, 0))] + out_specs
    return pl.pallas_call(
        functools.partial(_ln_mod_kernel, emit_xn=emit_xn),
        out_shape=out_shape,
        grid=(bsz, s // ts),
        in_specs=[pl.BlockSpec((1, ts, d), lambda i, j: (i, j, 0)),
                  pl.BlockSpec((1, d), lambda i, j: (0, 0)),
                  pl.BlockSpec((1, d), lambda i, j: (0, 0)),
                  pl.BlockSpec((1, 6, d), mod_map)],
        out_specs=out_specs,
        compiler_params=_cparams(("arbitrary", "arbitrary")),
        name="ln_mod_x" if emit_xn else "ln_mod_ctx",
    )(x, g, b, mod)


PROJ_ROWS = 256


def _proj_kernel(x_ref, w_ref, *rest, epi, scale, tn):
    o_ref = rest[-1]
    tm = x_ref.shape[0]
    rc = min(PROJ_ROWS, tm)
    for r in range(tm // rc):
        rs = slice(r * rc, (r + 1) * rc)
        acc = _dot(x_ref[rs, :], w_ref[...])
        if epi == "plain":
            if scale != 1.0:
                acc = acc * scale
            o_ref[rs, :] = acc.astype(o_ref.dtype)
        elif epi == "silu":
            o_ref[rs, :] = jax.nn.silu(acc).astype(o_ref.dtype)
        elif epi == "rope_ret":
            cos, sin = rest[0][0, rs, :], rest[1][0, rs, :]
            for h in range(tn // (2 * LANES)):
                a = h * 2 * LANES
                x1 = acc[:, a:a + LANES]
                x2 = acc[:, a + LANES:a + 2 * LANES]
                o_ref[rs, a:a + LANES] = (x1 * cos - x2 * sin).astype(o_ref.dtype)
                o_ref[rs, a + LANES:a + 2 * LANES] = (x2 * cos + x1 * sin).astype(o_ref.dtype)
        else:
            cos, sin = rest[0][0, rs, :], rest[1][0, rs, :]
            for g in range(tn // LANES):
                a = g * LANES
                xg = acc[:, a:a + LANES]
                o_ref[rs, a:a + LANES] = (xg * cos + pltpu.roll(xg, LANES // 2, axis=1) * sin).astype(o_ref.dtype)


def _proj_call(x, w, *, col0, ncols, out_dtype, epi="plain", scale=1.0, tables=None, seq=None, tm=2048, tn=1024, name="proj"):
    m, k = x.shape
    tm = min(tm, m if seq is None else seq)
    tn = min(tn, ncols)
    assert m % tm == 0 and ncols % tn == 0 and col0 % tn == 0
    jb = col0 // tn
    in_specs = [pl.BlockSpec((tm, k), lambda j, i: (i, 0)),
                pl.BlockSpec((k, tn), lambda j, i: (0, jb + j))]
    args = [x, w]
    if tables is not None:
        assert seq % tm == 0
        ns = seq // tm
        tspec = pl.BlockSpec((1, tm, LANES), lambda j, i: (j, i % ns, 0))
        in_specs += [tspec, tspec]
        args += list(tables)
    return pl.pallas_call(
        functools.partial(_proj_kernel, epi=epi, scale=scale, tn=tn),
        out_shape=jax.ShapeDtypeStruct((m, ncols), out_dtype),
        grid=(ncols // tn, m // tm),
        in_specs=in_specs,
        out_specs=pl.BlockSpec((tm, tn), lambda j, i: (i, j)),
        compiler_params=_cparams(("arbitrary", "arbitrary")),
        name=name,
    )(*args)


def _ret_kernel(lg_ref, q_ref, k_ref, v_ref, g_ref, kc_ref, vc_ref, o_ref, y_scr, sf_scr, sb_scr, *, chunk, n_chunks):
    c_len = chunk
    lg = lg_ref[0]
    lgf = lg[0:1, :]
    lgb = lg[1:2, :]

    def col(x):
        return x[:, 0:1]

    p_len = kc_ref.shape[1]
    pos_p = lax.broadcasted_iota(jnp.int32, (p_len, LANES), 0).astype(F32)
    kc = kc_ref[0]
    vc = vc_ref[0]
    sf_scr[...] = _dot_tn((kc * col(jnp.exp(((p_len - 1.0) - pos_p) * lgf))).astype(BF16), vc)
    sb_scr[...] = _dot_tn((kc * col(jnp.exp(pos_p * lgb))).astype(BF16), vc)

    pos = lax.broadcasted_iota(jnp.int32, (c_len, LANES), 0).astype(F32)
    dq_f = col(jnp.exp((pos + 1.0) * lgf))
    dk_f = col(jnp.exp((c_len - 1.0 - pos) * lgf))
    dq_b = col(jnp.exp((c_len - pos) * lgb))
    dk_b = col(jnp.exp(pos * lgb))
    ds_f = col(jnp.exp(c_len * lgf))
    ds_b = col(jnp.exp(c_len * lgb))
    rel = (lax.broadcasted_iota(jnp.int32, (c_len, c_len), 0)
           - lax.broadcasted_iota(jnp.int32, (c_len, c_len), 1)).astype(F32)
    d_in = (jnp.where(rel >= 0, jnp.exp(jnp.maximum(rel, 0.0) * col(lgf)), 0.0)
            + jnp.where(rel <= 0, jnp.exp(jnp.maximum(-rel, 0.0) * col(lgb)), 0.0))

    def rows(c):
        return pl.ds(pl.multiple_of(c * c_len, c_len), c_len)

    def fwd_part(c):
        q, k, v = q_ref[0, rows(c), :], k_ref[0, rows(c), :], v_ref[0, rows(c), :]
        inner = _dot_nt(q, k) * d_in
        s = sf_scr[...]
        y = _dot(inner.astype(BF16), v) + _dot(q, s.astype(BF16)) * dq_f
        sf_scr[...] = s * ds_f + _dot_tn((k.astype(F32) * dk_f).astype(BF16), v)
        return y

    def bwd_part(c):
        q, k, v = q_ref[0, rows(c), :], k_ref[0, rows(c), :], v_ref[0, rows(c), :]
        s = sb_scr[...]
        y = _dot(q, s.astype(BF16)) * dq_b
        sb_scr[...] = s * ds_b + _dot_tn((k.astype(F32) * dk_b).astype(BF16), v)
        return y

    def finish(c, y):
        mu = jnp.mean(y, axis=-1, keepdims=True)
        yc = y - mu
        var = jnp.mean(yc * yc, axis=-1, keepdims=True)
        gate = g_ref[0, rows(c), :].astype(F32)
        o_ref[0, rows(c), :] = (gate * (yc * lax.rsqrt(var + LN_EPS))).astype(o_ref.dtype)

    def first_half(i, carry):
        y_scr[rows(i), :] = fwd_part(i)
        y_scr[rows(n_chunks - 1 - i), :] = bwd_part(n_chunks - 1 - i)
        return carry

    def second_half(i, carry):
        cb = n_chunks - 1 - i
        finish(i, y_scr[rows(i), :] + fwd_part(i))
        finish(cb, y_scr[rows(cb), :] + bwd_part(cb))
        return carry

    lax.fori_loop(0, n_chunks // 2, first_half, 0, unroll=2)
    lax.fori_loop(n_chunks // 2, n_chunks, second_half, 0, unroll=2)


def _ret_call(lg_tab, qk, v, sg, krc, vrc):
    bsz, s, _ = qk.shape
    p_len = krc.shape[1]
    chunk = min(RET_CHUNK, s)
    assert s % (2 * chunk) == 0
    return pl.pallas_call(
        functools.partial(_ret_kernel, chunk=chunk, n_chunks=s // chunk),
        out_shape=jax.ShapeDtypeStruct((bsz, s, RET_HEADS * RET_DV), BF16),
        grid=(bsz, RET_HEADS),
        in_specs=[pl.BlockSpec((1, 8, LANES), lambda b, h: (h, 0, 0)),
                  pl.BlockSpec((1, s, RET_DK), lambda b, h: (b, 0, h)),
                  pl.BlockSpec((1, s, RET_DK), lambda b, h: (b, 0, RET_HEADS + h)),
                  pl.BlockSpec((1, s, RET_DV), lambda b, h: (b, 0, h)),
                  pl.BlockSpec((1, s, RET_DV), lambda b, h: (b, 0, h)),
                  pl.BlockSpec((1, p_len, RET_DK), lambda b, h: (b, 0, h)),
                  pl.BlockSpec((1, p_len, RET_DV), lambda b, h: (b, 0, h))],
        out_specs=pl.BlockSpec((1, s, RET_DV), lambda b, h: (b, 0, h)),
        scratch_shapes=[pltpu.VMEM((s, RET_DV), F32),
                        pltpu.VMEM((RET_DK, RET_DV), F32),
                        pltpu.VMEM((RET_DK, RET_DV), F32)],
        compiler_params=_cparams(("arbitrary", "arbitrary")),
        name="retention",
    )(lg_tab, qk, qk, v, sg, krc, vrc)


DIFF_TQI = 128
DIFF_TK = 256
ONES_ROWS = 16
NEG_BIG = -1e30


def _diff_kernel(dl_ref, gsub_ref, q_ref, k_ref, v_ref, kc_ref, vc_ref, o_ref, vt_scr, sa_scr, sb_scr, q2_scr,
                 *, s_len, p_len):
    tk = DIFF_TK
    n_lat, n_ctx = s_len // tk, p_len // tk
    n_blk = n_lat + n_ctx

    tqi = DIFF_TQI
    hd = 2 * DIFF_DH
    n_sub = s_len // tqi

    ones = jnp.ones((ONES_ROWS, tk), BF16)
    for j in range(n_blk):
        src = v_ref[0, j * tk:(j + 1) * tk, :] if j < n_lat else vc_ref[0, (j - n_lat) * tk:(j - n_lat + 1) * tk, :]
        vt_scr[j, 0:hd, :] = src.astype(F32).T.astype(BF16)
        vt_scr[j, hd:, :] = ones

    feat = lax.broadcasted_iota(jnp.int32, (hd, tqi), 0)
    comp0 = (feat & (hd // 4)) == 0

    def prep_q(t, carry):
        qt = q_ref[0, pl.ds(pl.multiple_of(t * tqi, tqi), tqi), :].astype(F32).T
        q2_scr[t] = jnp.concatenate([jnp.where(comp0, qt, 0.0), jnp.where(comp0, 0.0, qt)], axis=1).astype(BF16)
        return carry

    lax.fori_loop(0, n_sub, prep_q, 0, unroll=4)

    dl = dl_ref[...]
    lam = (jnp.exp(jnp.sum(dl[0:1] * dl[1:2], axis=-1, keepdims=True))
           - jnp.exp(jnp.sum(dl[2:3] * dl[3:4], axis=-1, keepdims=True)) + LAM_INIT)
    gsub = gsub_ref[...] * (1.0 - LAM_INIT)

    def key_block(j):
        return k_ref[0, j * tk:(j + 1) * tk, :] if j < n_lat else kc_ref[0, (j - n_lat) * tk:(j - n_lat + 1) * tk, :]

    def finish(t, acc):
        o2 = acc[0:hd] * (1.0 / acc[hd:hd + 1])
        o = (o2[:, :tqi] - lam * o2[:, tqi:]).T
        o = o * lax.rsqrt(jnp.mean(o * o, axis=-1, keepdims=True) + LN_EPS) * gsub
        row0 = t * tqi if isinstance(t, int) else pl.multiple_of(t * tqi, tqi)
        o_ref[0, pl.ds(row0, tqi), :] = o.astype(o_ref.dtype)

    def stage(t_val, s_val, mx, t_score, s_score, fin):
        mx_next = jnp.full((8, 2 * tqi), NEG_BIG, F32)
        acc = None
        if fin is not None:
            finish(*fin)
        if t_val is not None:
            m = jnp.max(mx, axis=0, keepdims=True)
            acc = jnp.zeros((hd + ONES_ROWS, 2 * tqi), F32)
        if t_score is not None:
            q2 = q2_scr[t_score]
        for j in range(n_blk):
            if t_val is not None:
                p = jnp.exp(s_val[j] - m)
                acc = acc + _dot(vt_scr[j], p.astype(BF16))
            if t_score is not None:
                s = _dot(key_block(j), q2)
                s_score[j] = s
                mx_next = jnp.maximum(mx_next, jnp.max(s.reshape(tk // 8, 8, 2 * tqi), axis=0))
        return mx_next, acc

    mx, _ = stage(None, None, None, 0, sa_scr, None)
    mx, acc = stage(0, sa_scr, mx, 1, sb_scr, None)

    def stage_pair(u, carry):
        mx, acc = carry
        mx, acc = stage(2 * u + 1, sb_scr, mx, 2 * u + 2, sa_scr, (2 * u, acc))
        return stage(2 * u + 2, sa_scr, mx, 2 * u + 3, sb_scr, (2 * u + 1, acc))

    mx, acc = lax.fori_loop(0, n_sub // 2 - 1, stage_pair, (mx, acc))
    _, acc_last = stage(n_sub - 1, sb_scr, mx, None, None, (n_sub - 2, acc))
    finish(n_sub - 1, acc_last)


def _diff_call(dl, gsub, qk, vgg, kdc, vdc):
    bsz, s, _ = qk.shape
    p_len = kdc.shape[1]
    hd = 2 * DIFF_DH
    n_blk = (s + p_len) // DIFF_TK
    assert s % DIFF_TK == 0 and p_len % DIFF_TK == 0 and s % (2 * DIFF_TQI) == 0
    return pl.pallas_call(
        functools.partial(_diff_kernel, s_len=s, p_len=p_len),
        out_shape=jax.ShapeDtypeStruct((bsz, s, D_MODEL), BF16),
        scratch_shapes=[pltpu.VMEM((n_blk, hd + ONES_ROWS, DIFF_TK), BF16),
                        pltpu.VMEM((n_blk, DIFF_TK, 2 * DIFF_TQI), F32),
                        pltpu.VMEM((n_blk, DIFF_TK, 2 * DIFF_TQI), F32),
                        pltpu.VMEM((s // DIFF_TQI, hd, 2 * DIFF_TQI), BF16)],
        grid=(bsz, DIFF_HEADS),
        in_specs=[pl.BlockSpec((4, DIFF_DH), lambda b, h: (0, 0)),
                  pl.BlockSpec((1, hd), lambda b, h: (0, 0)),
                  pl.BlockSpec((1, s, hd), lambda b, h: (b, 0, h)),
                  pl.BlockSpec((1, s, hd), lambda b, h: (b, 0, DIFF_HEADS + h)),
                  pl.BlockSpec((1, s, hd), lambda b, h: (b, 0, h)),
                  pl.BlockSpec((1, p_len, hd), lambda b, h: (b, 0, h)),
                  pl.BlockSpec((1, p_len, hd), lambda b, h: (b, 0, h))],
        out_specs=pl.BlockSpec((1, s, hd), lambda b, h: (b, 0, h)),
        compiler_params=_cparams(("arbitrary", "arbitrary")),
        name="diff_attn",
    )(dl, gsub, qk, qk, vgg, kdc, vdc)


def _merge_kernel(zr_ref, zd_ref, gr_ref, gd_ref, xn_ref, mod_ref, bg_ref, wr_ref, wd_ref, wo_ref, g1_ref, b1_ref,
                  x1_ref, h2_ref):
    bg = bg_ref[...]
    m = mod_ref[0]
    tm = xn_ref.shape[1]
    rc = min(PROJ_ROWS, tm)
    for r in range(tm // rc):
        rs = slice(r * rc, (r + 1) * rc)
        y_ret = _dot(zr_ref[0, rs, :], wr_ref[...])
        y_dif = _dot(zd_ref[0, rs, :], wd_ref[...])
        gate_r = jax.nn.sigmoid(gr_ref[0, rs, :].astype(F32) + bg[0:1])
        gate_d = jax.nn.sigmoid(gd_ref[0, rs, :].astype(F32) + bg[1:2])
        y_mix = _dot((gate_r * y_ret + gate_d * y_dif).astype(BF16), wo_ref[...])
        x1 = _layer_norm(ALPHA * xn_ref[0, rs, :] + m[2:3] * y_mix, g1_ref[...], b1_ref[...])
        x1_ref[0, rs, :] = x1
        h2_ref[0, rs, :] = (x1 * (1.0 + m[4:5]) + m[3:4]).astype(BF16)


def _merge_call(z_ret, z_dif, vgg, xn, mod, bg, w_ret, w_dif, w_o, g1, b1, *, tm=1024):
    bsz, s, d = xn.shape
    tm = min(tm, s)
    const = lambda b, i: (0, 0)
    return pl.pallas_call(
        _merge_kernel,
        out_shape=[jax.ShapeDtypeStruct((bsz, s, d), F32), jax.ShapeDtypeStruct((bsz, s, d), BF16)],
        grid=(bsz, s // tm),
        in_specs=[pl.BlockSpec((1, tm, RET_HEADS * RET_DV), lambda b, i: (b, i, 0)),
                  pl.BlockSpec((1, tm, d), lambda b, i: (b, i, 0)),
                  pl.BlockSpec((1, tm, d), lambda b, i: (b, i, 1)),
                  pl.BlockSpec((1, tm, d), lambda b, i: (b, i, 2)),
                  pl.BlockSpec((1, tm, d), lambda b, i: (b, i, 0)),
                  pl.BlockSpec((1, 6, d), lambda b, i: (b, 0, 0)),
                  pl.BlockSpec((2, d), const),
                  pl.BlockSpec(w_ret.shape, const, pipeline_mode=pl.Buffered(1)),
                  pl.BlockSpec(w_dif.shape, const, pipeline_mode=pl.Buffered(1)),
                  pl.BlockSpec(w_o.shape, const, pipeline_mode=pl.Buffered(1)),
                  pl.BlockSpec((1, d), const),
                  pl.BlockSpec((1, d), const)],
        out_specs=[pl.BlockSpec((1, tm, d), lambda b, i: (b, i, 0)),
                   pl.BlockSpec((1, tm, d), lambda b, i: (b, i, 0))],
        compiler_params=_cparams(("arbitrary", "arbitrary")),
        name="merge",
    )(z_ret, z_dif, vgg, vgg, xn, mod, bg, w_ret, w_dif, w_o, g1, b1)


HALO = 16
FF_CHUNK = 256


def _ffn_kernel(h_ref, hp_ref, hn_ref, x1_ref, mod_ref, wu_ref, cw_ref, cb_ref, wd_ref, g2_ref, b2_ref, o_ref,
                *, tm, n_tiles):
    i = pl.program_id(1)
    h = h_ref[0]
    h_prev = jnp.where(i > 0, hp_ref[0], jnp.zeros_like(hp_ref[0]))
    h_next = jnp.where(i < n_tiles - 1, hn_ref[0], jnp.zeros_like(hn_ref[0]))
    h_ext = jnp.concatenate([h_prev, h, h_next], axis=0)
    rows = tm + 2 * HALO
    acc = jnp.zeros((tm, D_MODEL), F32)
    for j in range(D_FF // FF_CHUNK):
        sl = slice(j * FF_CHUNK, (j + 1) * FF_CHUNK)
        u = _dot(h_ext, wu_ref[:, sl])
        gate = _dot(h, wu_ref[:, D_FF + j * FF_CHUNK:D_FF + (j + 1) * FF_CHUNK])
        cw = cw_ref[:, sl]
        t = (cb_ref[:, sl] + pltpu.roll(u, 1, axis=0)[HALO:HALO + tm] * cw[0:1] + u[HALO:HALO + tm] * cw[1:2]
             + pltpu.roll(u, rows - 1, axis=0)[HALO:HALO + tm] * cw[2:3])
        act = 0.5 * t * (1.0 + lax.erf(t * np.float32(np.sqrt(0.5)))) * gate
        acc = acc + _dot(act.astype(BF16), wd_ref[sl, :])
    m = mod_ref[0]
    o_ref[0] = _layer_norm(ALPHA * x1_ref[0] + m[5:6] * acc, g2_ref[...], b2_ref[...])


def _ffn_call(h2, x1, mod, w_up, conv_w, conv_b, w_down, g2, b2, *, tm=1024):
    bsz, s, d = x1.shape
    tm = min(tm, s)
    n_tiles = s // tm
    hb = tm // HALO
    n_halo = s // HALO
    const = lambda b, i: (0, 0)
    return pl.pallas_call(
        functools.partial(_ffn_kernel, tm=tm, n_tiles=n_tiles),
        out_shape=jax.ShapeDtypeStruct((bsz, s, d), F32),
        grid=(bsz, n_tiles),
        in_specs=[pl.BlockSpec((1, tm, d), lambda b, i: (b, i, 0)),
                  pl.BlockSpec((1, HALO, d), lambda b, i: (b, jnp.maximum(i * hb - 1, 0), 0)),
                  pl.BlockSpec((1, HALO, d), lambda b, i: (b, jnp.minimum((i + 1) * hb, n_halo - 1), 0)),
                  pl.BlockSpec((1, tm, d), lambda b, i: (b, i, 0)),
                  pl.BlockSpec((1, 6, d), lambda b, i: (b, 0, 0)),
                  pl.BlockSpec((d, 2 * D_FF), const, pipeline_mode=pl.Buffered(1)),
                  pl.BlockSpec((CONV_W, D_FF), const),
                  pl.BlockSpec((1, D_FF), const),
                  pl.BlockSpec((D_FF, d), const, pipeline_mode=pl.Buffered(1)),
                  pl.BlockSpec((1, d), const),
                  pl.BlockSpec((1, d), const)],
        out_specs=pl.BlockSpec((1, tm, d), lambda b, i: (b, i, 0)),
        compiler_params=_cparams(("arbitrary", "arbitrary")),
        name="ffn",
    )(h2, h2, h2, x1, mod, w_up, conv_w, conv_b, w_down, g2, b2)


def _axial_angles(s, head_dim):
    rows = s // GRID_W
    row = np.repeat(np.arange(rows, dtype=np.float64), GRID_W)
    col = np.tile(np.arange(GRID_W, dtype=np.float64), rows)
    half = head_dim // 2
    inv = ROPE_BASE ** (-(np.arange(0, half, 2, dtype=np.float64) / half))
    return np.concatenate([row[:, None] * inv, col[:, None] * inv], axis=-1)


def _as_tables(cos_pair, sin_pair):
    return jnp.asarray(np.stack(cos_pair), F32), jnp.asarray(np.stack(sin_pair), F32)


def _ret_tables(s):
    ang = _axial_angles(s, RET_DK)
    cos, sin = np.cos(ang), np.sin(ang)
    k_scale = RET_DK ** -0.5
    return _as_tables([cos, cos * k_scale], [sin, sin * k_scale])


def _dif_tables(s):
    ang = _axial_angles(s, DIFF_DH)
    cos, sin = np.cos(ang), np.sin(ang)
    cos_t = np.tile(cos, (1, 4))
    sin_t = np.concatenate([-sin, -sin, sin, sin], axis=-1)
    q_scale = DIFF_DH ** -0.5
    return _as_tables([cos_t * q_scale, cos_t], [sin_t * q_scale, sin_t])


def _dif_col_perm():
    q4 = DIFF_DH // 2
    g = np.concatenate([np.arange(0, q4), np.arange(2 * q4, 3 * q4), np.arange(q4, 2 * q4), np.arange(3 * q4, 4 * q4)])
    return np.concatenate([h * LANES + g for h in range(2 * DIFF_HEADS)])


def kernel(x, c, ctx, c_ctx, ln_in_g, ln_in_b, w_mod, b_mod, w_in, b_gate, ret_decay_logit, diff_lambda, diff_subln_g,
           w_ret_out, w_diff_out, w_o, ln1_g, ln1_b, w_up, conv_w, conv_b, w_down, ln2_g, ln2_b):
    bsz, s, d = x.shape
    p_len = ctx.shape[1]
    row2 = lambda v: v.reshape(1, -1)

    n_rows = ((bsz + 1 + 7) // 8) * 8
    cc = jnp.zeros((n_rows, d), F32).at[:bsz].set(c).at[bsz].set(c_ctx)
    mod = _mod_call(cc, w_mod[0], row2(b_mod[0])).reshape(n_rows, 6, d)

    xn, h1 = _ln_mod_call(x, row2(ln_in_g), row2(ln_in_b), mod, mod_row=None, emit_xn=True, ts=512)
    (hc,) = _ln_mod_call(ctx, row2(ln_in_g), row2(ln_in_b), mod, mod_row=bsz, emit_xn=False, ts=256)

    w = w_in[0]
    w_qk_r = w[:, OFF_QR:OFF_VR].astype(BF16)
    w_v_r = w[:, OFF_VR:OFF_GR].astype(BF16)
    w_g_r = w[:, OFF_GR:OFF_QD].astype(BF16)
    w_qk_d = w[:, OFF_QD:OFF_VD][:, _dif_col_perm()].astype(BF16)
    w_vgg = w[:, OFF_VD:].astype(BF16)

    h1f = h1.reshape(bsz * s, d)
    hcf = hc.reshape(bsz * p_len, d)
    qk_r = _proj_call(h1f, w_qk_r, col0=0, ncols=2 * D_MODEL, out_dtype=BF16, epi="rope_ret",
                      tables=_ret_tables(s), seq=s, name="proj_qk_ret").reshape(bsz, s, -1)
    v_r = _proj_call(h1f, w_v_r, col0=0, ncols=RET_HEADS * RET_DV, out_dtype=BF16,
                     name="proj_v_ret").reshape(bsz, s, -1)
    sg_r = _proj_call(h1f, w_g_r, col0=0, ncols=RET_HEADS * RET_DV, out_dtype=BF16, epi="silu",
                      name="proj_g_ret").reshape(bsz, s, -1)
    qk_d = _proj_call(h1f, w_qk_d, col0=0, ncols=2 * D_MODEL, out_dtype=BF16, epi="rope_dif",
                      tables=_dif_tables(s), seq=s, name="proj_qk_dif").reshape(bsz, s, -1)
    vgg = _proj_call(h1f, w_vgg, col0=0, ncols=3 * D_MODEL, out_dtype=BF16,
                     name="proj_v_gates").reshape(bsz, s, -1)
    kr_c = _proj_call(hcf, w_qk_r, col0=D_MODEL, ncols=D_MODEL, out_dtype=F32, scale=RET_DK ** -0.5,
                      name="proj_ctx_kr").reshape(bsz, p_len, -1)
    vr_c = _proj_call(hcf, w_v_r, col0=0, ncols=RET_HEADS * RET_DV, out_dtype=BF16,
                      name="proj_ctx_vr").reshape(bsz, p_len, -1)
    kd_c = _proj_call(hcf, w_qk_d, col0=D_MODEL, ncols=D_MODEL, out_dtype=BF16,
                      name="proj_ctx_kd").reshape(bsz, p_len, -1)
    vd_c = _proj_call(hcf, w_vgg, col0=0, ncols=D_MODEL, out_dtype=BF16,
                      name="proj_ctx_vd").reshape(bsz, p_len, -1)

    log_g = jax.nn.log_sigmoid(ret_decay_logit[0].astype(F32))
    lg_tab = jnp.zeros((RET_HEADS, 8, LANES), F32).at[:, :2, :].set(
        jnp.broadcast_to(log_g.T[:, :, None], (RET_HEADS, 2, LANES)))
    z_ret = _ret_call(lg_tab, qk_r, v_r, sg_r, kr_c, vr_c)
    z_dif = _diff_call(diff_lambda[0].astype(F32), row2(diff_subln_g[0]), qk_d, vgg, kd_c, vd_c)

    x1, h2 = _merge_call(z_ret, z_dif, vgg, xn, mod, b_gate[0].reshape(2, d), w_ret_out[0].astype(BF16),
                         w_diff_out[0].astype(BF16), w_o[0].astype(BF16), row2(ln1_g[0]), row2(ln1_b[0]))

    return _ffn_call(h2, x1, mod, w_up[0].astype(BF16), conv_w[0], row2(conv_b[0]), w_down[0].astype(BF16),
                     row2(ln2_g[0]), row2(ln2_b[0]))
```

```python
import functools
import math

import numpy as np
import jax
import jax.numpy as jnp
from jax import lax
from jax.experimental import pallas as pl
from jax.experimental.pallas import tpu as pltpu

F32 = jnp.float32
BF16 = jnp.bfloat16

D_MODEL = 1024
GRID_W = 64
RET_HEADS = 4
RET_DK = D_MODEL // RET_HEADS
RET_DV = 2 * RET_DK
DIFF_DH = 64
DIFF_HEADS = D_MODEL // (2 * DIFF_DH)
D_FF = ((8 * D_MODEL // 3 + 127) // 128) * 128
CONV_W = 3
ROPE_BASE = 10000.0
LN_EPS = 1e-5
DEPTH = 1
ALPHA = (2.0 * DEPTH) ** 0.25
LAM_INIT = 0.8 - 0.6 * math.exp(-0.3 * 0)

OFF_QR, OFF_KR, OFF_VR, OFF_GR = 0, RET_HEADS * RET_DK, 2 * RET_HEADS * RET_DK, 2 * RET_HEADS * RET_DK + RET_HEADS * RET_DV
OFF_QD = OFF_GR + RET_HEADS * RET_DV
OFF_KD = OFF_QD + D_MODEL
OFF_VD = OFF_KD + D_MODEL
OFF_GATE_R = OFF_VD + D_MODEL
OFF_GATE_D = OFF_GATE_R + D_MODEL
N_IN = OFF_GATE_D + D_MODEL

LANES = 128
RET_CHUNK = 256
VMEM_LIMIT = 60 * 1024 * 1024


def _cparams(sem):
    return pltpu.CompilerParams(dimension_semantics=sem, vmem_limit_bytes=VMEM_LIMIT)


def _layer_norm(x, g, b):
    mu = jnp.mean(x, axis=-1, keepdims=True)
    xc = x - mu
    var = jnp.mean(xc * xc, axis=-1, keepdims=True)
    return xc * lax.rsqrt(var + LN_EPS) * g + b


def _dot(a, b):
    return jnp.dot(a, b, preferred_element_type=F32)


def _dot_nt(a, b):
    return lax.dot_general(a, b, (((1,), (1,)), ((), ())), preferred_element_type=F32)


def _dot_tn(a, b):
    return lax.dot_general(a, b, (((0,), (0,)), ((), ())), preferred_element_type=F32)


def _mod_kernel(c_ref, w_ref, b_ref, o_ref):
    cond = jax.nn.silu(c_ref[...])
    o_ref[...] = _dot(cond.astype(BF16), w_ref[...].astype(BF16)) + b_ref[...]


def _mod_call(cc, w_mod, b_mod):
    rows, d = cc.shape
    n = w_mod.shape[1]
    tn = 1024
    return pl.pallas_call(
        _mod_kernel,
        out_shape=jax.ShapeDtypeStruct((rows, n), F32),
        grid=(n // tn,),
        in_specs=[pl.BlockSpec((rows, d), lambda j: (0, 0)),
                  pl.BlockSpec((d, tn), lambda j: (0, j)),
                  pl.BlockSpec((1, tn), lambda j: (0, j))],
        out_specs=pl.BlockSpec((rows, tn), lambda j: (0, j)),
        compiler_params=_cparams(("arbitrary",)),
        name="mod",
    )(cc, w_mod, b_mod)


def _ln_mod_kernel(x_ref, g_ref, b_ref, mod_ref, h_ref):
    xn = _layer_norm(x_ref[0], g_ref[...], b_ref[...])
    m = mod_ref[0]
    h_ref[0] = (xn * (1.0 + m[1:2]) + m[0:1]).astype(BF16)


def _ln_mod_call(x, g, b, mod, *, mod_row, ts, name):
    bsz, s, d = x.shape
    ts = min(ts, s)
    mod_map = (lambda i, j: (i, 0, 0)) if mod_row is None else (lambda i, j: (mod_row, 0, 0))
    return pl.pallas_call(
        _ln_mod_kernel,
        out_shape=jax.ShapeDtypeStruct((bsz, s, d), BF16),
        grid=(bsz, s // ts),
        in_specs=[pl.BlockSpec((1, ts, d), lambda i, j: (i, j, 0)),
                  pl.BlockSpec((1, d), lambda i, j: (0, 0)),
                  pl.BlockSpec((1, d), lambda i, j: (0, 0)),
                  pl.BlockSpec((1, 6, d), mod_map)],
        out_specs=pl.BlockSpec((1, ts, d), lambda i, j: (i, j, 0)),
        compiler_params=_cparams(("arbitrary", "arbitrary")),
        name=name,
    )(x, g, b, mod)


PROJ_ROWS = 256


def _proj_kernel(x_ref, w_ref, *rest, epi, scale, tn):
    o_ref = rest[-1]
    tm = x_ref.shape[0]
    rc = min(PROJ_ROWS, tm)
    for r in range(tm // rc):
        rs = slice(r * rc, (r + 1) * rc)
        acc = _dot(x_ref[rs, :], w_ref[...])
        if epi == "plain":
            if scale != 1.0:
                acc = acc * scale
            o_ref[rs, :] = acc.astype(o_ref.dtype)
        elif epi == "silu":
            o_ref[rs, :] = jax.nn.silu(acc).astype(o_ref.dtype)
        elif epi == "rope_ret":
            cos, sin = rest[0][0, rs, :], rest[1][0, rs, :]
            for h in range(tn // (2 * LANES)):
                a = h * 2 * LANES
                x1 = acc[:, a:a + LANES]
                x2 = acc[:, a + LANES:a + 2 * LANES]
                o_ref[rs, a:a + LANES] = (x1 * cos - x2 * sin).astype(o_ref.dtype)
                o_ref[rs, a + LANES:a + 2 * LANES] = (x2 * cos + x1 * sin).astype(o_ref.dtype)
        else:
            cos, sin = rest[0][0, rs, :], rest[1][0, rs, :]
            for g in range(tn // LANES):
                a = g * LANES
                xg = acc[:, a:a + LANES]
                o_ref[rs, a:a + LANES] = (xg * cos + pltpu.roll(xg, LANES // 2, axis=1) * sin).astype(o_ref.dtype)


def _proj_call(x, w, *, col0, ncols, out_dtype, epi="plain", scale=1.0, tables=None, seq=None, tm=2048, tn=1024, name="proj"):
    m, k = x.shape
    tm = min(tm, m if seq is None else seq)
    tn = min(tn, ncols)
    assert m % tm == 0 and ncols % tn == 0 and col0 % tn == 0
    jb = col0 // tn
    in_specs = [pl.BlockSpec((tm, k), lambda j, i: (i, 0)),
                pl.BlockSpec((k, tn), lambda j, i: (0, jb + j))]
    args = [x, w]
    if tables is not None:
        assert seq % tm == 0
        ns = seq // tm
        tspec = pl.BlockSpec((1, tm, LANES), lambda j, i: (j, i % ns, 0))
        in_specs += [tspec, tspec]
        args += list(tables)
    return pl.pallas_call(
        functools.partial(_proj_kernel, epi=epi, scale=scale, tn=tn),
        out_shape=jax.ShapeDtypeStruct((m, ncols), out_dtype),
        grid=(ncols // tn, m // tm),
        in_specs=in_specs,
        out_specs=pl.BlockSpec((tm, tn), lambda j, i: (i, j)),
        compiler_params=_cparams(("arbitrary", "arbitrary")),
        name=name,
    )(*args)


def _ret_kernel(lg_ref, q_ref, k_ref, v_ref, g_ref, kc_ref, vc_ref, o_ref, y_scr, sf_scr, sb_scr, *, chunk, n_chunks):
    c_len = chunk
    lg = lg_ref[0]
    lgf = lg[0:1, :]
    lgb = lg[1:2, :]

    def col(x):
        return x[:, 0:1]

    p_len = kc_ref.shape[1]
    pos_p = lax.broadcasted_iota(jnp.int32, (p_len, LANES), 0).astype(F32)
    kc = kc_ref[0]
    vc = vc_ref[0]
    sf_scr[...] = _dot_tn((kc * col(jnp.exp(((p_len - 1.0) - pos_p) * lgf))).astype(BF16), vc)
    sb_scr[...] = _dot_tn((kc * col(jnp.exp(pos_p * lgb))).astype(BF16), vc)

    pos = lax.broadcasted_iota(jnp.int32, (c_len, LANES), 0).astype(F32)
    dq_f = col(jnp.exp((pos + 1.0) * lgf))
    dk_f = col(jnp.exp((c_len - 1.0 - pos) * lgf))
    dq_b = col(jnp.exp((c_len - pos) * lgb))
    dk_b = col(jnp.exp(pos * lgb))
    ds_f = col(jnp.exp(c_len * lgf))
    ds_b = col(jnp.exp(c_len * lgb))
    rel = (lax.broadcasted_iota(jnp.int32, (c_len, c_len), 0)
           - lax.broadcasted_iota(jnp.int32, (c_len, c_len), 1)).astype(F32)
    d_in = (jnp.where(rel >= 0, jnp.exp(jnp.maximum(rel, 0.0) * col(lgf)), 0.0)
            + jnp.where(rel <= 0, jnp.exp(jnp.maximum(-rel, 0.0) * col(lgb)), 0.0))

    def rows(c):
        return pl.ds(pl.multiple_of(c * c_len, c_len), c_len)

    def fwd_part(c):
        q, k, v = q_ref[0, rows(c), :], k_ref[0, rows(c), :], v_ref[0, rows(c), :]
        inner = _dot_nt(q, k) * d_in
        s = sf_scr[...]
        y = _dot(inner.astype(BF16), v) + _dot(q, s.astype(BF16)) * dq_f
        sf_scr[...] = s * ds_f + _dot_tn((k.astype(F32) * dk_f).astype(BF16), v)
        return y

    def bwd_part(c):
        q, k, v = q_ref[0, rows(c), :], k_ref[0, rows(c), :], v_ref[0, rows(c), :]
        s = sb_scr[...]
        y = _dot(q, s.astype(BF16)) * dq_b
        sb_scr[...] = s * ds_b + _dot_tn((k.astype(F32) * dk_b).astype(BF16), v)
        return y

    def finish(c, y):
        mu = jnp.mean(y, axis=-1, keepdims=True)
        yc = y - mu
        var = jnp.mean(yc * yc, axis=-1, keepdims=True)
        gate = g_ref[0, rows(c), :].astype(F32)
        o_ref[0, rows(c), :] = (gate * (yc * lax.rsqrt(var + LN_EPS))).astype(o_ref.dtype)

    def first_half(i, carry):
        y_scr[rows(i), :] = fwd_part(i)
        y_scr[rows(n_chunks - 1 - i), :] = bwd_part(n_chunks - 1 - i)
        return carry

    def second_half(i, carry):
        cb = n_chunks - 1 - i
        finish(i, y_scr[rows(i), :] + fwd_part(i))
        finish(cb, y_scr[rows(cb), :] + bwd_part(cb))
        return carry

    lax.fori_loop(0, n_chunks // 2, first_half, 0, unroll=2)
    lax.fori_loop(n_chunks // 2, n_chunks, second_half, 0, unroll=2)


def _ret_call(lg_tab, qk, v, sg, krc, vrc):
    bsz, s, _ = qk.shape
    p_len = krc.shape[1]
    chunk = min(RET_CHUNK, s)
    assert s % (2 * chunk) == 0
    return pl.pallas_call(
        functools.partial(_ret_kernel, chunk=chunk, n_chunks=s // chunk),
        out_shape=jax.ShapeDtypeStruct((bsz, s, RET_HEADS * RET_DV), BF16),
        grid=(bsz, RET_HEADS),
        in_specs=[pl.BlockSpec((1, 8, LANES), lambda b, h: (h, 0, 0)),
                  pl.BlockSpec((1, s, RET_DK), lambda b, h: (b, 0, h)),
                  pl.BlockSpec((1, s, RET_DK), lambda b, h: (b, 0, RET_HEADS + h)),
                  pl.BlockSpec((1, s, RET_DV), lambda b, h: (b, 0, h)),
                  pl.BlockSpec((1, s, RET_DV), lambda b, h: (b, 0, h)),
                  pl.BlockSpec((1, p_len, RET_DK), lambda b, h: (b, 0, h)),
                  pl.BlockSpec((1, p_len, RET_DV), lambda b, h: (b, 0, h))],
        out_specs=pl.BlockSpec((1, s, RET_DV), lambda b, h: (b, 0, h)),
        scratch_shapes=[pltpu.VMEM((s, RET_DV), F32),
                        pltpu.VMEM((RET_DK, RET_DV), F32),
                        pltpu.VMEM((RET_DK, RET_DV), F32)],
        compiler_params=_cparams(("arbitrary", "arbitrary")),
        name="retention",
    )(lg_tab, qk, qk, v, sg, krc, vrc)


DIFF_TQI = 128
DIFF_TK = 256
DIFF_GROUP = 4
ONES_ROWS = 16
NEG_BIG = -1e30


def _diff_kernel(dl_ref, gsub_ref, q_ref, k_ref, v_ref, kc_ref, vc_ref, o_ref, vt_scr, sa_scr, sb_scr, q2_scr,
                 *, s_len, p_len):
    tk = min(DIFF_TK, s_len)
    blocks = [(0, r, r, tk) for r in range(0, s_len, tk)] + [(1, 0, s_len, p_len)]

    tqi = DIFF_TQI
    hd = 2 * DIFF_DH
    n_sub = s_len // tqi

    for src, r, a, n in blocks:
        v_blk = (vc_ref if src else v_ref)[0, r:r + n, :]
        vt_scr[0:hd, a:a + n] = v_blk.astype(F32).T.astype(BF16)
    vt_scr[hd:, :] = jnp.ones((ONES_ROWS, s_len + p_len), BF16)

    feat = lax.broadcasted_iota(jnp.int32, (hd, tqi), 0)
    comp0 = (feat & (hd // 4)) == 0

    def prep_q(t, carry):
        qt = q_ref[0, pl.ds(pl.multiple_of(t * tqi, tqi), tqi), :].astype(F32).T
        q2_scr[t] = jnp.concatenate([jnp.where(comp0, qt, 0.0), jnp.where(comp0, 0.0, qt)], axis=1).astype(BF16)
        return carry

    lax.fori_loop(0, n_sub, prep_q, 0, unroll=4)

    dl = dl_ref[...]
    lam = (jnp.exp(jnp.sum(dl[0:1] * dl[1:2], axis=-1, keepdims=True))
           - jnp.exp(jnp.sum(dl[2:3] * dl[3:4], axis=-1, keepdims=True)) + LAM_INIT)
    gsub = gsub_ref[...] * (1.0 - LAM_INIT)


    def finish(t, acc):
        o2 = acc[0:hd] * (1.0 / acc[hd:hd + 1])
        o = (o2[:, :tqi] - lam * o2[:, tqi:]).T
        o = o * lax.rsqrt(jnp.mean(o * o, axis=-1, keepdims=True) + LN_EPS) * gsub
        row0 = t * tqi if isinstance(t, int) else pl.multiple_of(t * tqi, tqi)
        o_ref[0, pl.ds(row0, tqi), :] = o.astype(o_ref.dtype)

    def stage(t_val, s_val, mx, t_score, s_score, fin):
        mx_next = jnp.full((8, 2 * tqi), NEG_BIG, F32)
        acc = None
        if fin is not None:
            finish(*fin)
        if t_val is not None:
            m = jnp.max(mx, axis=0, keepdims=True)
            acc = jnp.zeros((hd + ONES_ROWS, 2 * tqi), F32)
        if t_score is not None:
            q2 = q2_scr[t_score]
        for src, r, a, n in blocks:
            if t_val is not None:
                p = jnp.exp(s_val[a:a + n, :] - m)
                acc = acc + _dot(vt_scr[:, a:a + n], p.astype(BF16))
            if t_score is not None:
                s = _dot((kc_ref if src else k_ref)[0, r:r + n, :], q2)
                s_score[a:a + n, :] = s
                mx_next = jnp.maximum(mx_next, jnp.max(s.reshape(n // 8, 8, 2 * tqi), axis=0))
        return mx_next, acc

    def full_stage(t, odd, mx, acc):
        s_val, s_score = (sb_scr, sa_scr) if odd else (sa_scr, sb_scr)
        return stage(t, s_val, mx, t + 1, s_score, (t - 1, acc))

    mx, _ = stage(None, None, None, 0, sa_scr, None)
    mx, acc = stage(0, sa_scr, mx, 1, sb_scr, None)
    mx, acc = full_stage(1, True, mx, acc)
    mx, acc = full_stage(2, False, mx, acc)

    def stage_group(u, carry):
        mx, acc = carry
        for i in range(DIFF_GROUP):
            mx, acc = full_stage(DIFF_GROUP * u + 3 + i, i % 2 == 0, mx, acc)
        return mx, acc

    mx, acc = lax.fori_loop(0, (n_sub - 4) // DIFF_GROUP, stage_group, (mx, acc))
    _, acc_last = stage(n_sub - 1, sb_scr, mx, None, None, (n_sub - 2, acc))
    finish(n_sub - 1, acc_last)


def _diff_call(dl, gsub, qk, vgg, kdc, vdc):
    bsz, s, _ = qk.shape
    p_len = kdc.shape[1]
    hd = 2 * DIFF_DH
    n_keys = s + p_len
    assert s % min(DIFF_TK, s) == 0 and p_len % (2 * LANES) == 0 and s % (DIFF_GROUP * DIFF_TQI) == 0
    return pl.pallas_call(
        functools.partial(_diff_kernel, s_len=s, p_len=p_len),
        out_shape=jax.ShapeDtypeStruct((bsz, s, D_MODEL), BF16),
        scratch_shapes=[pltpu.VMEM((hd + ONES_ROWS, n_keys), BF16),
                        pltpu.VMEM((n_keys, 2 * DIFF_TQI), F32),
                        pltpu.VMEM((n_keys, 2 * DIFF_TQI), F32),
                        pltpu.VMEM((s // DIFF_TQI, hd, 2 * DIFF_TQI), BF16)],
        grid=(bsz, DIFF_HEADS),
        in_specs=[pl.BlockSpec((4, DIFF_DH), lambda b, h: (0, 0)),
                  pl.BlockSpec((1, hd), lambda b, h: (0, 0)),
                  pl.BlockSpec((1, s, hd), lambda b, h: (b, 0, h)),
                  pl.BlockSpec((1, s, hd), lambda b, h: (b, 0, DIFF_HEADS + h)),
                  pl.BlockSpec((1, s, hd), lambda b, h: (b, 0, h)),
                  pl.BlockSpec((1, p_len, hd), lambda b, h: (b, 0, h)),
                  pl.BlockSpec((1, p_len, hd), lambda b, h: (b, 0, h))],
        out_specs=pl.BlockSpec((1, s, hd), lambda b, h: (b, 0, h)),
        compiler_params=_cparams(("arbitrary", "arbitrary")),
        name="diff_attn",
    )(dl, gsub, qk, qk, vgg, kdc, vdc)


def _merge_kernel(zr_ref, zd_ref, gr_ref, gd_ref, x_ref, mod_ref, bg_ref, wr_ref, wd_ref, wo_ref, gi_ref, bi_ref,
                  g1_ref, b1_ref, x1_ref, h2_ref):
    bg = bg_ref[...]
    m = mod_ref[0]
    tm = x_ref.shape[1]
    rc = min(PROJ_ROWS, tm)
    for r in range(tm // rc):
        rs = slice(r * rc, (r + 1) * rc)
        y_ret = _dot(zr_ref[0, rs, :], wr_ref[...])
        y_dif = _dot(zd_ref[0, rs, :], wd_ref[...])
        gate_r = jax.nn.sigmoid(gr_ref[0, rs, :].astype(F32) + bg[0:1])
        gate_d = jax.nn.sigmoid(gd_ref[0, rs, :].astype(F32) + bg[1:2])
        y_mix = _dot((gate_r * y_ret + gate_d * y_dif).astype(BF16), wo_ref[...])
        xn = _layer_norm(x_ref[0, rs, :], gi_ref[...], bi_ref[...])
        x1 = _layer_norm(ALPHA * xn + m[2:3] * y_mix, g1_ref[...], b1_ref[...])
        x1_ref[0, rs, :] = x1
        h2_ref[0, rs, :] = (x1 * (1.0 + m[4:5]) + m[3:4]).astype(BF16)


def _merge_call(z_ret, z_dif, vgg, x, mod, bg, w_ret, w_dif, w_o, gi, bi, g1, b1, *, tm=1024):
    bsz, s, d = x.shape
    tm = min(tm, s)
    const = lambda b, i: (0, 0)
    return pl.pallas_call(
        _merge_kernel,
        out_shape=[jax.ShapeDtypeStruct((bsz, s, d), F32), jax.ShapeDtypeStruct((bsz, s, d), BF16)],
        grid=(bsz, s // tm),
        in_specs=[pl.BlockSpec((1, tm, RET_HEADS * RET_DV), lambda b, i: (b, i, 0)),
                  pl.BlockSpec((1, tm, d), lambda b, i: (b, i, 0)),
                  pl.BlockSpec((1, tm, d), lambda b, i: (b, i, 1)),
                  pl.BlockSpec((1, tm, d), lambda b, i: (b, i, 2)),
                  pl.BlockSpec((1, tm, d), lambda b, i: (b, i, 0)),
                  pl.BlockSpec((1, 6, d), lambda b, i: (b, 0, 0)),
                  pl.BlockSpec((2, d), const),
                  pl.BlockSpec(w_ret.shape, const, pipeline_mode=pl.Buffered(1)),
                  pl.BlockSpec(w_dif.shape, const, pipeline_mode=pl.Buffered(1)),
                  pl.BlockSpec(w_o.shape, const, pipeline_mode=pl.Buffered(1)),
                  pl.BlockSpec((1, d), const),
                  pl.BlockSpec((1, d), const),
                  pl.BlockSpec((1, d), const),
                  pl.BlockSpec((1, d), const)],
        out_specs=[pl.BlockSpec((1, tm, d), lambda b, i: (b, i, 0)),
                   pl.BlockSpec((1, tm, d), lambda b, i: (b, i, 0))],
        compiler_params=_cparams(("arbitrary", "arbitrary")),
        name="merge",
    )(z_ret, z_dif, vgg, vgg, x, mod, bg, w_ret, w_dif, w_o, gi, bi, g1, b1)


HALO = 16
FF_CHUNK = 256
FF_GROUP = 6


def _ffn_kernel(h_ref, hp_ref, hn_ref, x1_ref, mod_ref, wu_ref, cw_ref, cb_ref, wd_ref, g2_ref, b2_ref, o_ref,
                *, tm, n_tiles):
    i = pl.program_id(1)
    h = h_ref[0]
    h_prev = jnp.where(i > 0, hp_ref[0], jnp.zeros_like(hp_ref[0]))
    h_next = jnp.where(i < n_tiles - 1, hn_ref[0], jnp.zeros_like(hn_ref[0]))
    h_ext = jnp.concatenate([h_prev, h, h_next], axis=0)
    rows = tm + 2 * HALO
    acc = jnp.zeros((tm, D_MODEL), F32)
    n_chunks = D_FF // FF_CHUNK
    acts = []
    for j in range(n_chunks):
        sl = slice(j * FF_CHUNK, (j + 1) * FF_CHUNK)
        u = _dot(h_ext, wu_ref[:, sl])
        gate = _dot(h, wu_ref[:, D_FF + j * FF_CHUNK:D_FF + (j + 1) * FF_CHUNK])
        cw = cw_ref[:, sl]
        t = (cb_ref[:, sl] + pltpu.roll(u, 1, axis=0)[HALO:HALO + tm] * cw[0:1] + u[HALO:HALO + tm] * cw[1:2]
             + pltpu.roll(u, rows - 1, axis=0)[HALO:HALO + tm] * cw[2:3])
        act = 0.5 * t * (1.0 + lax.erf(t * np.float32(np.sqrt(0.5)))) * gate
        acts.append(act.astype(BF16))
        if len(acts) == FF_GROUP or j == n_chunks - 1:
            k0 = (j + 1 - len(acts)) * FF_CHUNK
            acc = acc + _dot(jnp.concatenate(acts, axis=1), wd_ref[k0:(j + 1) * FF_CHUNK, :])
            acts = []
    m = mod_ref[0]
    o_ref[0] = _layer_norm(ALPHA * x1_ref[0] + m[5:6] * acc, g2_ref[...], b2_ref[...])


def _ffn_call(h2, x1, mod, w_up, conv_w, conv_b, w_down, g2, b2, *, tm=1024):
    bsz, s, d = x1.shape
    tm = min(tm, s)
    n_tiles = s // tm
    hb = tm // HALO
    n_halo = s // HALO
    const = lambda b, i: (0, 0)
    return pl.pallas_call(
        functools.partial(_ffn_kernel, tm=tm, n_tiles=n_tiles),
        out_shape=jax.ShapeDtypeStruct((bsz, s, d), F32),
        grid=(bsz, n_tiles),
        in_specs=[pl.BlockSpec((1, tm, d), lambda b, i: (b, i, 0)),
                  pl.BlockSpec((1, HALO, d), lambda b, i: (b, jnp.maximum(i * hb - 1, 0), 0)),
                  pl.BlockSpec((1, HALO, d), lambda b, i: (b, jnp.minimum((i + 1) * hb, n_halo - 1), 0)),
                  pl.BlockSpec((1, tm, d), lambda b, i: (b, i, 0)),
                  pl.BlockSpec((1, 6, d), lambda b, i: (b, 0, 0)),
                  pl.BlockSpec((d, 2 * D_FF), const, pipeline_mode=pl.Buffered(1)),
                  pl.BlockSpec((CONV_W, D_FF), const),
                  pl.BlockSpec((1, D_FF), const),
                  pl.BlockSpec((D_FF, d), const, pipeline_mode=pl.Buffered(1)),
                  pl.BlockSpec((1, d), const),
                  pl.BlockSpec((1, d), const)],
        out_specs=pl.BlockSpec((1, tm, d), lambda b, i: (b, i, 0)),
        compiler_params=_cparams(("arbitrary", "arbitrary")),
        name="ffn",
    )(h2, h2, h2, x1, mod, w_up, conv_w, conv_b, w_down, g2, b2)


def _axial_angles(s, head_dim):
    rows = s // GRID_W
    row = np.repeat(np.arange(rows, dtype=np.float64), GRID_W)
    col = np.tile(np.arange(GRID_W, dtype=np.float64), rows)
    half = head_dim // 2
    inv = ROPE_BASE ** (-(np.arange(0, half, 2, dtype=np.float64) / half))
    return np.concatenate([row[:, None] * inv, col[:, None] * inv], axis=-1)


def _as_tables(cos_pair, sin_pair):
    return jnp.asarray(np.stack(cos_pair), F32), jnp.asarray(np.stack(sin_pair), F32)


def _ret_tables(s):
    ang = _axial_angles(s, RET_DK)
    cos, sin = np.cos(ang), np.sin(ang)
    k_scale = RET_DK ** -0.5
    return _as_tables([cos, cos * k_scale], [sin, sin * k_scale])


def _dif_tables(s):
    ang = _axial_angles(s, DIFF_DH)
    cos, sin = np.cos(ang), np.sin(ang)
    cos_t = np.tile(cos, (1, 4))
    sin_t = np.concatenate([-sin, -sin, sin, sin], axis=-1)
    q_scale = DIFF_DH ** -0.5
    return _as_tables([cos_t * q_scale, cos_t], [sin_t * q_scale, sin_t])


def _dif_col_perm():
    q4 = DIFF_DH // 2
    g = np.concatenate([np.arange(0, q4), np.arange(2 * q4, 3 * q4), np.arange(q4, 2 * q4), np.arange(3 * q4, 4 * q4)])
    return np.concatenate([h * LANES + g for h in range(2 * DIFF_HEADS)])


def kernel(x, c, ctx, c_ctx, ln_in_g, ln_in_b, w_mod, b_mod, w_in, b_gate, ret_decay_logit, diff_lambda, diff_subln_g,
           w_ret_out, w_diff_out, w_o, ln1_g, ln1_b, w_up, conv_w, conv_b, w_down, ln2_g, ln2_b):
    bsz, s, d = x.shape
    p_len = ctx.shape[1]
    row2 = lambda v: v.reshape(1, -1)

    n_rows = ((bsz + 1 + 7) // 8) * 8
    cc = jnp.zeros((n_rows, d), F32).at[:bsz].set(c).at[bsz].set(c_ctx)
    mod = _mod_call(cc, w_mod[0], row2(b_mod[0])).reshape(n_rows, 6, d)

    h1 = _ln_mod_call(x, row2(ln_in_g), row2(ln_in_b), mod, mod_row=None, ts=1024, name="ln_mod_x")
    hc = _ln_mod_call(ctx, row2(ln_in_g), row2(ln_in_b), mod, mod_row=bsz, ts=256, name="ln_mod_ctx")

    w = w_in[0]
    w_qk_r = w[:, OFF_QR:OFF_VR].astype(BF16)
    w_v_r = w[:, OFF_VR:OFF_GR].astype(BF16)
    w_g_r = w[:, OFF_GR:OFF_QD].astype(BF16)
    w_qk_d = w[:, OFF_QD:OFF_VD][:, _dif_col_perm()].astype(BF16)
    w_vgg = w[:, OFF_VD:].astype(BF16)

    h1f = h1.reshape(bsz * s, d)
    hcf = hc.reshape(bsz * p_len, d)
    qk_r = _proj_call(h1f, w_qk_r, col0=0, ncols=2 * D_MODEL, out_dtype=BF16, epi="rope_ret",
                      tables=_ret_tables(s), seq=s, name="proj_qk_ret").reshape(bsz, s, -1)
    v_r = _proj_call(h1f, w_v_r, col0=0, ncols=RET_HEADS * RET_DV, out_dtype=BF16,
                     name="proj_v_ret").reshape(bsz, s, -1)
    sg_r = _proj_call(h1f, w_g_r, col0=0, ncols=RET_HEADS * RET_DV, out_dtype=BF16, epi="silu",
                      name="proj_g_ret").reshape(bsz, s, -1)
    qk_d = _proj_call(h1f, w_qk_d, col0=0, ncols=2 * D_MODEL, out_dtype=BF16, epi="rope_dif",
                      tables=_dif_tables(s), seq=s, name="proj_qk_dif").reshape(bsz, s, -1)
    vgg = _proj_call(h1f, w_vgg, col0=0, ncols=3 * D_MODEL, out_dtype=BF16,
                     name="proj_v_gates").reshape(bsz, s, -1)
    kr_c = _proj_call(hcf, w_qk_r, col0=D_MODEL, ncols=D_MODEL, out_dtype=F32, scale=RET_DK ** -0.5,
                      name="proj_ctx_kr").reshape(bsz, p_len, -1)
    vr_c = _proj_call(hcf, w_v_r, col0=0, ncols=RET_HEADS * RET_DV, out_dtype=BF16,
                      name="proj_ctx_vr").reshape(bsz, p_len, -1)
    kd_c = _proj_call(hcf, w_qk_d, col0=D_MODEL, ncols=D_MODEL, out_dtype=BF16,
                      name="proj_ctx_kd").reshape(bsz, p_len, -1)
    vd_c = _proj_call(hcf, w_vgg, col0=0, ncols=D_MODEL, out_dtype=BF16,
                      name="proj_ctx_vd").reshape(bsz, p_len, -1)

    log_g = jax.nn.log_sigmoid(ret_decay_logit[0].astype(F32))
    lg_tab = jnp.zeros((RET_HEADS, 8, LANES), F32).at[:, :2, :].set(
        jnp.broadcast_to(log_g.T[:, :, None], (RET_HEADS, 2, LANES)))
    z_ret = _ret_call(lg_tab, qk_r, v_r, sg_r, kr_c, vr_c)
    z_dif = _diff_call(diff_lambda[0].astype(F32), row2(diff_subln_g[0]), qk_d, vgg, kd_c, vd_c)

    x1, h2 = _merge_call(z_ret, z_dif, vgg, x, mod, b_gate[0].reshape(2, d), w_ret_out[0].astype(BF16),
                         w_diff_out[0].astype(BF16), w_o[0].astype(BF16), row2(ln_in_g), row2(ln_in_b),
                         row2(ln1_g[0]), row2(ln1_b[0]))

    return _ffn_call(h2, x1, mod, w_up[0].astype(BF16), conv_w[0], row2(conv_b[0]), w_down[0].astype(BF16),
                     row2(ln2_g[0]), row2(ln2_b[0]))
```

```python
import functools
import math

import numpy as np
import jax
import jax.numpy as jnp
from jax import lax
from jax.experimental import pallas as pl
from jax.experimental.pallas import tpu as pltpu

F32 = jnp.float32
BF16 = jnp.bfloat16

D_MODEL = 1024
GRID_W = 64
RET_HEADS = 4
RET_DK = D_MODEL // RET_HEADS
RET_DV = 2 * RET_DK
DIFF_DH = 64
DIFF_HEADS = D_MODEL // (2 * DIFF_DH)
D_FF = ((8 * D_MODEL // 3 + 127) // 128) * 128
CONV_W = 3
ROPE_BASE = 10000.0
LN_EPS = 1e-5
DEPTH = 1
ALPHA = (2.0 * DEPTH) ** 0.25
LAM_INIT = 0.8 - 0.6 * math.exp(-0.3 * 0)

OFF_QR, OFF_KR, OFF_VR, OFF_GR = 0, RET_HEADS * RET_DK, 2 * RET_HEADS * RET_DK, 2 * RET_HEADS * RET_DK + RET_HEADS * RET_DV
OFF_QD = OFF_GR + RET_HEADS * RET_DV
OFF_KD = OFF_QD + D_MODEL
OFF_VD = OFF_KD + D_MODEL
OFF_GATE_R = OFF_VD + D_MODEL
OFF_GATE_D = OFF_GATE_R + D_MODEL
N_IN = OFF_GATE_D + D_MODEL

LANES = 128
RET_CHUNK = 256
VMEM_LIMIT = 60 * 1024 * 1024


def _cparams(sem):
    return pltpu.CompilerParams(dimension_semantics=sem, vmem_limit_bytes=VMEM_LIMIT)


def _layer_norm(x, g, b):
    mu = jnp.mean(x, axis=-1, keepdims=True)
    xc = x - mu
    var = jnp.mean(xc * xc, axis=-1, keepdims=True)
    return xc * lax.rsqrt(var + LN_EPS) * g + b


def _dot(a, b):
    return jnp.dot(a, b, preferred_element_type=F32)


def _dot_nt(a, b):
    return lax.dot_general(a, b, (((1,), (1,)), ((), ())), preferred_element_type=F32)


def _dot_tn(a, b):
    return lax.dot_general(a, b, (((0,), (0,)), ((), ())), preferred_element_type=F32)


def _mod_kernel(c_ref, w_ref, b_ref, o_ref):
    cond = jax.nn.silu(c_ref[...])
    o_ref[...] = _dot(cond.astype(BF16), w_ref[...].astype(BF16)) + b_ref[...]


def _mod_call(cc, w_mod, b_mod):
    rows, d = cc.shape
    n = w_mod.shape[1]
    tn = 1024
    return pl.pallas_call(
        _mod_kernel,
        out_shape=jax.ShapeDtypeStruct((rows, n), F32),
        grid=(n // tn,),
        in_specs=[pl.BlockSpec((rows, d), lambda j: (0, 0)),
                  pl.BlockSpec((d, tn), lambda j: (0, j)),
                  pl.BlockSpec((1, tn), lambda j: (0, j))],
        out_specs=pl.BlockSpec((rows, tn), lambda j: (0, j)),
        compiler_params=_cparams(("arbitrary",)),
        name="mod",
    )(cc, w_mod, b_mod)


def _ln_mod_kernel(x_ref, g_ref, b_ref, mod_ref, h_ref):
    xn = _layer_norm(x_ref[0], g_ref[...], b_ref[...])
    m = mod_ref[0]
    h_ref[0] = (xn * (1.0 + m[1:2]) + m[0:1]).astype(BF16)


def _ln_mod_call(x, g, b, mod, *, mod_row, ts, name):
    bsz, s, d = x.shape
    ts = min(ts, s)
    mod_map = (lambda i, j: (i, 0, 0)) if mod_row is None else (lambda i, j: (mod_row, 0, 0))
    return pl.pallas_call(
        _ln_mod_kernel,
        out_shape=jax.ShapeDtypeStruct((bsz, s, d), BF16),
        grid=(bsz, s // ts),
        in_specs=[pl.BlockSpec((1, ts, d), lambda i, j: (i, j, 0)),
                  pl.BlockSpec((1, d), lambda i, j: (0, 0)),
                  pl.BlockSpec((1, d), lambda i, j: (0, 0)),
                  pl.BlockSpec((1, 6, d), mod_map)],
        out_specs=pl.BlockSpec((1, ts, d), lambda i, j: (i, j, 0)),
        compiler_params=_cparams(("arbitrary", "arbitrary")),
        name=name,
    )(x, g, b, mod)


PROJ_ROWS = 256


def _proj_kernel(x_ref, w_ref, *rest, epi, scale, tn):
    o_ref = rest[-1]
    tm = x_ref.shape[0]
    rc = min(PROJ_ROWS, tm)
    for r in range(tm // rc):
        rs = slice(r * rc, (r + 1) * rc)
        acc = _dot(x_ref[rs, :], w_ref[...])
        if epi == "plain":
            if scale != 1.0:
                acc = acc * scale
            o_ref[rs, :] = acc.astype(o_ref.dtype)
        elif epi == "silu":
            o_ref[rs, :] = jax.nn.silu(acc).astype(o_ref.dtype)
        elif epi == "rope_ret":
            cos, sin = rest[0][0, rs, :], rest[1][0, rs, :]
            for h in range(tn // (2 * LANES)):
                a = h * 2 * LANES
                x1 = acc[:, a:a + LANES]
                x2 = acc[:, a + LANES:a + 2 * LANES]
                o_ref[rs, a:a + LANES] = (x1 * cos - x2 * sin).astype(o_ref.dtype)
                o_ref[rs, a + LANES:a + 2 * LANES] = (x2 * cos + x1 * sin).astype(o_ref.dtype)
        else:
            cos, sin = rest[0][0, rs, :], rest[1][0, rs, :]
            for g in range(tn // LANES):
                a = g * LANES
                xg = acc[:, a:a + LANES]
                o_ref[rs, a:a + LANES] = (xg * cos + pltpu.roll(xg, LANES // 2, axis=1) * sin).astype(o_ref.dtype)


def _proj_call(x, w, *, col0, ncols, out_dtype, epi="plain", scale=1.0, tables=None, seq=None, tm=2048, tn=1024, name="proj"):
    m, k = x.shape
    tm = min(tm, m if seq is None else seq)
    tn = min(tn, ncols)
    assert m % tm == 0 and ncols % tn == 0 and col0 % tn == 0
    jb = col0 // tn
    in_specs = [pl.BlockSpec((tm, k), lambda j, i: (i, 0)),
                pl.BlockSpec((k, tn), lambda j, i: (0, jb + j))]
    args = [x, w]
    if tables is not None:
        assert seq % tm == 0
        ns = seq // tm
        tspec = pl.BlockSpec((1, tm, LANES), lambda j, i: (j, i % ns, 0))
        in_specs += [tspec, tspec]
        args += list(tables)
    return pl.pallas_call(
        functools.partial(_proj_kernel, epi=epi, scale=scale, tn=tn),
        out_shape=jax.ShapeDtypeStruct((m, ncols), out_dtype),
        grid=(ncols // tn, m // tm),
        in_specs=in_specs,
        out_specs=pl.BlockSpec((tm, tn), lambda j, i: (i, j)),
        compiler_params=_cparams(("arbitrary", "arbitrary")),
        name=name,
    )(*args)


def _ret_kernel(lg_ref, q_ref, k_ref, v_ref, g_ref, kc_ref, vc_ref, o_ref, y_scr, sf_scr, sb_scr, *, chunk, n_chunks):
    c_len = chunk
    lg = lg_ref[0]
    lgf = lg[0:1, :]
    lgb = lg[1:2, :]

    def col(x):
        return x[:, 0:1]

    p_len = kc_ref.shape[1]
    pos_p = lax.broadcasted_iota(jnp.int32, (p_len, LANES), 0).astype(F32)
    kc = kc_ref[0]
    vc = vc_ref[0]
    sf_scr[...] = _dot_tn((kc * col(jnp.exp(((p_len - 1.0) - pos_p) * lgf))).astype(BF16), vc)
    sb_scr[...] = _dot_tn((kc * col(jnp.exp(pos_p * lgb))).astype(BF16), vc)

    pos = lax.broadcasted_iota(jnp.int32, (c_len, LANES), 0).astype(F32)
    dq_f = col(jnp.exp((pos + 1.0) * lgf))
    dk_f = col(jnp.exp((c_len - 1.0 - pos) * lgf))
    dq_b = col(jnp.exp((c_len - pos) * lgb))
    dk_b = col(jnp.exp(pos * lgb))
    ds_f = col(jnp.exp(c_len * lgf))
    ds_b = col(jnp.exp(c_len * lgb))
    rel = (lax.broadcasted_iota(jnp.int32, (c_len, c_len), 0)
           - lax.broadcasted_iota(jnp.int32, (c_len, c_len), 1)).astype(F32)
    d_in = (jnp.where(rel >= 0, jnp.exp(jnp.maximum(rel, 0.0) * col(lgf)), 0.0)
            + jnp.where(rel <= 0, jnp.exp(jnp.maximum(-rel, 0.0) * col(lgb)), 0.0))

    def rows(c):
        return pl.ds(pl.multiple_of(c * c_len, c_len), c_len)

    def fwd_part(c):
        q, k, v = q_ref[0, rows(c), :], k_ref[0, rows(c), :], v_ref[0, rows(c), :]
        inner = _dot_nt(q, k) * d_in
        s = sf_scr[...]
        y = _dot(inner.astype(BF16), v) + _dot(q, s.astype(BF16)) * dq_f
        sf_scr[...] = s * ds_f + _dot_tn((k.astype(F32) * dk_f).astype(BF16), v)
        return y

    def bwd_part(c):
        q, k, v = q_ref[0, rows(c), :], k_ref[0, rows(c), :], v_ref[0, rows(c), :]
        s = sb_scr[...]
        y = _dot(q, s.astype(BF16)) * dq_b
        sb_scr[...] = s * ds_b + _dot_tn((k.astype(F32) * dk_b).astype(BF16), v)
        return y

    def finish(c, y):
        mu = jnp.mean(y, axis=-1, keepdims=True)
        yc = y - mu
        var = jnp.mean(yc * yc, axis=-1, keepdims=True)
        gate = g_ref[0, rows(c), :].astype(F32)
        o_ref[0, rows(c), :] = (gate * (yc * lax.rsqrt(var + LN_EPS))).astype(o_ref.dtype)

    def first_half(i, carry):
        y_scr[rows(i), :] = fwd_part(i)
        y_scr[rows(n_chunks - 1 - i), :] = bwd_part(n_chunks - 1 - i)
        return carry

    def second_half(i, carry):
        cb = n_chunks - 1 - i
        finish(i, y_scr[rows(i), :] + fwd_part(i))
        finish(cb, y_scr[rows(cb), :] + bwd_part(cb))
        return carry

    lax.fori_loop(0, n_chunks // 2, first_half, 0, unroll=2)
    lax.fori_loop(n_chunks // 2, n_chunks, second_half, 0, unroll=2)


def _ret_call(lg_tab, qk, v, sg, krc, vrc):
    bsz, s, _ = qk.shape
    p_len = krc.shape[1]
    chunk = min(RET_CHUNK, s)
    assert s % (2 * chunk) == 0
    return pl.pallas_call(
        functools.partial(_ret_kernel, chunk=chunk, n_chunks=s // chunk),
        out_shape=jax.ShapeDtypeStruct((bsz, s, RET_HEADS * RET_DV), BF16),
        grid=(bsz, RET_HEADS),
        in_specs=[pl.BlockSpec((1, 8, LANES), lambda b, h: (h, 0, 0)),
                  pl.BlockSpec((1, s, RET_DK), lambda b, h: (b, 0, h)),
                  pl.BlockSpec((1, s, RET_DK), lambda b, h: (b, 0, RET_HEADS + h)),
                  pl.BlockSpec((1, s, RET_DV), lambda b, h: (b, 0, h)),
                  pl.BlockSpec((1, s, RET_DV), lambda b, h: (b, 0, h)),
                  pl.BlockSpec((1, p_len, RET_DK), lambda b, h: (b, 0, h)),
                  pl.BlockSpec((1, p_len, RET_DV), lambda b, h: (b, 0, h))],
        out_specs=pl.BlockSpec((1, s, RET_DV), lambda b, h: (b, 0, h)),
        scratch_shapes=[pltpu.VMEM((s, RET_DV), F32),
                        pltpu.VMEM((RET_DK, RET_DV), F32),
                        pltpu.VMEM((RET_DK, RET_DV), F32)],
        compiler_params=_cparams(("arbitrary", "arbitrary")),
        name="retention",
    )(lg_tab, qk, qk, v, sg, krc, vrc)


DIFF_TQI = 128
DIFF_TK = 256
DIFF_GROUP = 4
ONES_ROWS = 16
NEG_BIG = -1e30


def _diff_kernel(dl_ref, gsub_ref, q_ref, k_ref, v_ref, kc_ref, vc_ref, o_ref, vt_scr, sa_scr, sb_scr, q2_scr,
                 *, s_len, p_len):
    tk = min(DIFF_TK, s_len)
    blocks = [(0, r, r, tk) for r in range(0, s_len, tk)] + [(1, 0, s_len, p_len)]

    tqi = DIFF_TQI
    hd = 2 * DIFF_DH
    n_sub = s_len // tqi

    for src, r, a, n in blocks:
        v_blk = (vc_ref if src else v_ref)[0, r:r + n, :]
        vt_scr[0:hd, a:a + n] = v_blk.astype(F32).T.astype(BF16)
    vt_scr[hd:, :] = jnp.ones((ONES_ROWS, s_len + p_len), BF16)

    feat = lax.broadcasted_iota(jnp.int32, (hd, tqi), 0)
    comp0 = (feat & (hd // 4)) == 0

    def prep_q(t):
        if isinstance(t, int):
            if t >= n_sub:
                return
            row0 = t * tqi
        else:
            t = jnp.minimum(t, n_sub - 1)
            row0 = pl.multiple_of(t * tqi, tqi)
        qt = q_ref[0, pl.ds(row0, tqi), :].astype(F32).T
        q2_scr[t] = jnp.concatenate([jnp.where(comp0, qt, 0.0), jnp.where(comp0, 0.0, qt)], axis=1).astype(BF16)

    prep_q(0)

    dl = dl_ref[...]
    lam = (jnp.exp(jnp.sum(dl[0:1] * dl[1:2], axis=-1, keepdims=True))
           - jnp.exp(jnp.sum(dl[2:3] * dl[3:4], axis=-1, keepdims=True)) + LAM_INIT)
    gsub = gsub_ref[...] * (1.0 - LAM_INIT)


    def finish(t, acc):
        o2 = acc[0:hd] * (1.0 / acc[hd:hd + 1])
        o = (o2[:, :tqi] - lam * o2[:, tqi:]).T
        o = o * lax.rsqrt(jnp.mean(o * o, axis=-1, keepdims=True) + LN_EPS) * gsub
        row0 = t * tqi if isinstance(t, int) else pl.multiple_of(t * tqi, tqi)
        o_ref[0, pl.ds(row0, tqi), :] = o.astype(o_ref.dtype)

    def stage(t_val, s_val, mx, t_score, s_score, fin):
        mx_next = jnp.full((8, 2 * tqi), NEG_BIG, F32)
        acc = None
        if fin is not None:
            finish(*fin)
        if t_val is not None:
            m = jnp.max(mx, axis=0, keepdims=True)
            acc = jnp.zeros((hd + ONES_ROWS, 2 * tqi), F32)
        if t_score is not None:
            q2 = q2_scr[t_score]
        for src, r, a, n in blocks:
            if t_val is not None:
                p = jnp.exp(s_val[a:a + n, :] - m)
                acc = acc + _dot(vt_scr[:, a:a + n], p.astype(BF16))
            if t_score is not None:
                s = _dot((kc_ref if src else k_ref)[0, r:r + n, :], q2)
                s_score[a:a + n, :] = s
                mx_next = jnp.maximum(mx_next, jnp.max(s.reshape(n // 8, 8, 2 * tqi), axis=0))
        if t_score is not None:
            prep_q(t_score + 1)
        return mx_next, acc

    def full_stage(t, odd, mx, acc):
        s_val, s_score = (sb_scr, sa_scr) if odd else (sa_scr, sb_scr)
        return stage(t, s_val, mx, t + 1, s_score, (t - 1, acc))

    mx, _ = stage(None, None, None, 0, sa_scr, None)
    mx, acc = stage(0, sa_scr, mx, 1, sb_scr, None)
    mx, acc = full_stage(1, True, mx, acc)
    mx, acc = full_stage(2, False, mx, acc)

    def stage_group(u, carry):
        mx, acc = carry
        for i in range(group):
            mx, acc = full_stage(group * u + 3 + i, i % 2 == 0, mx, acc)
        return mx, acc

    group = max(g for g in range(2, DIFF_GROUP + 1, 2) if (n_sub - 4) % g == 0)
    mx, acc = lax.fori_loop(0, (n_sub - 4) // group, stage_group, (mx, acc))
    _, acc_last = stage(n_sub - 1, sb_scr, mx, None, None, (n_sub - 2, acc))
    finish(n_sub - 1, acc_last)


def _diff_call(dl, gsub, qk, vgg, kdc, vdc):
    bsz, s, _ = qk.shape
    p_len = kdc.shape[1]
    hd = 2 * DIFF_DH
    n_keys = s + p_len
    assert s % min(DIFF_TK, s) == 0 and p_len % (2 * LANES) == 0 and s % (4 * DIFF_TQI) == 0
    return pl.pallas_call(
        functools.partial(_diff_kernel, s_len=s, p_len=p_len),
        out_shape=jax.ShapeDtypeStruct((bsz, s, D_MODEL), BF16),
        scratch_shapes=[pltpu.VMEM((hd + ONES_ROWS, n_keys), BF16),
                        pltpu.VMEM((n_keys, 2 * DIFF_TQI), F32),
                        pltpu.VMEM((n_keys, 2 * DIFF_TQI), F32),
                        pltpu.VMEM((s // DIFF_TQI, hd, 2 * DIFF_TQI), BF16)],
        grid=(bsz, DIFF_HEADS),
        in_specs=[pl.BlockSpec((4, DIFF_DH), lambda b, h: (0, 0)),
                  pl.BlockSpec((1, hd), lambda b, h: (0, 0)),
                  pl.BlockSpec((1, s, hd), lambda b, h: (b, 0, h)),
                  pl.BlockSpec((1, s, hd), lambda b, h: (b, 0, DIFF_HEADS + h)),
                  pl.BlockSpec((1, s, hd), lambda b, h: (b, 0, h)),
                  pl.BlockSpec((1, p_len, hd), lambda b, h: (b, 0, h)),
                  pl.BlockSpec((1, p_len, hd), lambda b, h: (b, 0, h))],
        out_specs=pl.BlockSpec((1, s, hd), lambda b, h: (b, 0, h)),
        compiler_params=_cparams(("arbitrary", "arbitrary")),
        name="diff_attn",
    )(dl, gsub, qk, qk, vgg, kdc, vdc)


MERGE_ROWS = 512


def _merge_kernel(zr_ref, zd_ref, gr_ref, gd_ref, x_ref, mod_ref, bg_ref, wr_ref, wd_ref, wo_ref, gi_ref, bi_ref,
                  g1_ref, b1_ref, x1_ref, h2_ref):
    bg = bg_ref[...]
    m = mod_ref[0]
    tm = x_ref.shape[1]
    rc = min(MERGE_ROWS, tm)
    for r in range(tm // rc):
        rs = slice(r * rc, (r + 1) * rc)
        y_ret = _dot(zr_ref[0, rs, :], wr_ref[...])
        y_dif = _dot(zd_ref[0, rs, :], wd_ref[...])
        gate_r = jax.nn.sigmoid(gr_ref[0, rs, :].astype(F32) + bg[0:1])
        gate_d = jax.nn.sigmoid(gd_ref[0, rs, :].astype(F32) + bg[1:2])
        y_mix = _dot((gate_r * y_ret + gate_d * y_dif).astype(BF16), wo_ref[...])
        xn = _layer_norm(x_ref[0, rs, :], gi_ref[...], bi_ref[...])
        x1 = _layer_norm(ALPHA * xn + m[2:3] * y_mix, g1_ref[...], b1_ref[...])
        x1_ref[0, rs, :] = x1
        h2_ref[0, rs, :] = (x1 * (1.0 + m[4:5]) + m[3:4]).astype(BF16)


def _merge_call(z_ret, z_dif, vgg, x, mod, bg, w_ret, w_dif, w_o, gi, bi, g1, b1, *, tm=1024):
    bsz, s, d = x.shape
    tm = min(tm, s)
    const = lambda b, i: (0, 0)
    return pl.pallas_call(
        _merge_kernel,
        out_shape=[jax.ShapeDtypeStruct((bsz, s, d), F32), jax.ShapeDtypeStruct((bsz, s, d), BF16)],
        grid=(bsz, s // tm),
        in_specs=[pl.BlockSpec((1, tm, RET_HEADS * RET_DV), lambda b, i: (b, i, 0)),
                  pl.BlockSpec((1, tm, d), lambda b, i: (b, i, 0)),
                  pl.BlockSpec((1, tm, d), lambda b, i: (b, i, 1)),
                  pl.BlockSpec((1, tm, d), lambda b, i: (b, i, 2)),
                  pl.BlockSpec((1, tm, d), lambda b, i: (b, i, 0)),
                  pl.BlockSpec((1, 6, d), lambda b, i: (b, 0, 0)),
                  pl.BlockSpec((2, d), const),
                  pl.BlockSpec(w_ret.shape, const, pipeline_mode=pl.Buffered(1)),
                  pl.BlockSpec(w_dif.shape, const, pipeline_mode=pl.Buffered(1)),
                  pl.BlockSpec(w_o.shape, const, pipeline_mode=pl.Buffered(1)),
                  pl.BlockSpec((1, d), const),
                  pl.BlockSpec((1, d), const),
                  pl.BlockSpec((1, d), const),
                  pl.BlockSpec((1, d), const)],
        out_specs=[pl.BlockSpec((1, tm, d), lambda b, i: (b, i, 0)),
                   pl.BlockSpec((1, tm, d), lambda b, i: (b, i, 0))],
        compiler_params=_cparams(("arbitrary", "arbitrary")),
        name="merge",
    )(z_ret, z_dif, vgg, vgg, x, mod, bg, w_ret, w_dif, w_o, gi, bi, g1, b1)


HALO = 16
FF_CHUNK = 256
FF_GROUP = 6


def _ffn_kernel(h_ref, hp_ref, hn_ref, x1_ref, mod_ref, wu_ref, cw_ref, cb_ref, wd_ref, g2_ref, b2_ref, o_ref,
                *, tm, n_tiles):
    i = pl.program_id(1)
    h = h_ref[0]
    h_prev = jnp.where(i > 0, hp_ref[0], jnp.zeros_like(hp_ref[0]))
    h_next = jnp.where(i < n_tiles - 1, hn_ref[0], jnp.zeros_like(hn_ref[0]))
    h_ext = jnp.concatenate([h_prev, h, h_next], axis=0)
    rows = tm + 2 * HALO
    acc = jnp.zeros((tm, D_MODEL), F32)
    n_chunks = D_FF // FF_CHUNK
    acts = []
    for j in range(n_chunks):
        sl = slice(j * FF_CHUNK, (j + 1) * FF_CHUNK)
        u = _dot(h_ext, wu_ref[:, sl])
        gate = _dot(h, wu_ref[:, D_FF + j * FF_CHUNK:D_FF + (j + 1) * FF_CHUNK])
        cw = cw_ref[:, sl]
        t = (cb_ref[:, sl] + pltpu.roll(u, 1, axis=0)[HALO:HALO + tm] * cw[0:1] + u[HALO:HALO + tm] * cw[1:2]
             + pltpu.roll(u, rows - 1, axis=0)[HALO:HALO + tm] * cw[2:3])
        act = 0.5 * t * (1.0 + lax.erf(t * np.float32(np.sqrt(0.5)))) * gate
        acts.append(act.astype(BF16))
        if len(acts) == FF_GROUP or j == n_chunks - 1:
            k0 = (j + 1 - len(acts)) * FF_CHUNK
            acc = acc + _dot(jnp.concatenate(acts, axis=1), wd_ref[k0:(j + 1) * FF_CHUNK, :])
            acts = []
    m = mod_ref[0]
    o_ref[0] = _layer_norm(ALPHA * x1_ref[0] + m[5:6] * acc, g2_ref[...], b2_ref[...])


def _ffn_call(h2, x1, mod, w_up, conv_w, conv_b, w_down, g2, b2, *, tm=1024):
    bsz, s, d = x1.shape
    tm = min(tm, s)
    n_tiles = s // tm
    hb = tm // HALO
    n_halo = s // HALO
    const = lambda b, i: (0, 0)
    return pl.pallas_call(
        functools.partial(_ffn_kernel, tm=tm, n_tiles=n_tiles),
        out_shape=jax.ShapeDtypeStruct((bsz, s, d), F32),
        grid=(bsz, n_tiles),
        in_specs=[pl.BlockSpec((1, tm, d), lambda b, i: (b, i, 0)),
                  pl.BlockSpec((1, HALO, d), lambda b, i: (b, jnp.maximum(i * hb - 1, 0), 0)),
                  pl.BlockSpec((1, HALO, d), lambda b, i: (b, jnp.minimum((i + 1) * hb, n_halo - 1), 0)),
                  pl.BlockSpec((1, tm, d), lambda b, i: (b, i, 0)),
                  pl.BlockSpec((1, 6, d), lambda b, i: (b, 0, 0)),
                  pl.BlockSpec((d, 2 * D_FF), const, pipeline_mode=pl.Buffered(1)),
                  pl.BlockSpec((CONV_W, D_FF), const),
                  pl.BlockSpec((1, D_FF), const),
                  pl.BlockSpec((D_FF, d), const, pipeline_mode=pl.Buffered(1)),
                  pl.BlockSpec((1, d), const),
                  pl.BlockSpec((1, d), const)],
        out_specs=pl.BlockSpec((1, tm, d), lambda b, i: (b, i, 0)),
        compiler_params=_cparams(("arbitrary", "arbitrary")),
        name="ffn",
    )(h2, h2, h2, x1, mod, w_up, conv_w, conv_b, w_down, g2, b2)


def _axial_angles(s, head_dim):
    rows = s // GRID_W
    row = np.repeat(np.arange(rows, dtype=np.float64), GRID_W)
    col = np.tile(np.arange(GRID_W, dtype=np.float64), rows)
    half = head_dim // 2
    inv = ROPE_BASE ** (-(np.arange(0, half, 2, dtype=np.float64) / half))
    return np.concatenate([row[:, None] * inv, col[:, None] * inv], axis=-1)


def _as_tables(cos_pair, sin_pair):
    return jnp.asarray(np.stack(cos_pair), F32), jnp.asarray(np.stack(sin_pair), F32)


def _ret_tables(s):
    ang = _axial_angles(s, RET_DK)
    cos, sin = np.cos(ang), np.sin(ang)
    k_scale = RET_DK ** -0.5
    return _as_tables([cos, cos * k_scale], [sin, sin * k_scale])


def _dif_tables(s):
    ang = _axial_angles(s, DIFF_DH)
    cos, sin = np.cos(ang), np.sin(ang)
    cos_t = np.tile(cos, (1, 4))
    sin_t = np.concatenate([-sin, -sin, sin, sin], axis=-1)
    q_scale = DIFF_DH ** -0.5
    return _as_tables([cos_t * q_scale, cos_t], [sin_t * q_scale, sin_t])


def _dif_col_perm():
    q4 = DIFF_DH // 2
    g = np.concatenate([np.arange(0, q4), np.arange(2 * q4, 3 * q4), np.arange(q4, 2 * q4), np.arange(3 * q4, 4 * q4)])
    return np.concatenate([h * LANES + g for h in range(2 * DIFF_HEADS)])


def kernel(x, c, ctx, c_ctx, ln_in_g, ln_in_b, w_mod, b_mod, w_in, b_gate, ret_decay_logit, diff_lambda, diff_subln_g,
           w_ret_out, w_diff_out, w_o, ln1_g, ln1_b, w_up, conv_w, conv_b, w_down, ln2_g, ln2_b):
    bsz, s, d = x.shape
    p_len = ctx.shape[1]
    row2 = lambda v: v.reshape(1, -1)

    n_rows = ((bsz + 1 + 7) // 8) * 8
    cc = jnp.zeros((n_rows, d), F32).at[:bsz].set(c).at[bsz].set(c_ctx)
    mod = _mod_call(cc, w_mod[0], row2(b_mod[0])).reshape(n_rows, 6, d)

    h1 = _ln_mod_call(x, row2(ln_in_g), row2(ln_in_b), mod, mod_row=None, ts=1024, name="ln_mod_x")
    hc = _ln_mod_call(ctx, row2(ln_in_g), row2(ln_in_b), mod, mod_row=bsz, ts=256, name="ln_mod_ctx")

    w = w_in[0]
    w_qk_r = w[:, OFF_QR:OFF_VR].astype(BF16)
    w_v_r = w[:, OFF_VR:OFF_GR].astype(BF16)
    w_g_r = w[:, OFF_GR:OFF_QD].astype(BF16)
    w_qk_d = w[:, OFF_QD:OFF_VD][:, _dif_col_perm()].astype(BF16)
    w_vgg = w[:, OFF_VD:].astype(BF16)

    h1f = h1.reshape(bsz * s, d)
    hcf = hc.reshape(bsz * p_len, d)
    qk_r = _proj_call(h1f, w_qk_r, col0=0, ncols=2 * D_MODEL, out_dtype=BF16, epi="rope_ret",
                      tables=_ret_tables(s), seq=s, name="proj_qk_ret").reshape(bsz, s, -1)
    v_r = _proj_call(h1f, w_v_r, col0=0, ncols=RET_HEADS * RET_DV, out_dtype=BF16,
                     name="proj_v_ret").reshape(bsz, s, -1)
    sg_r = _proj_call(h1f, w_g_r, col0=0, ncols=RET_HEADS * RET_DV, out_dtype=BF16, epi="silu",
                      name="proj_g_ret").reshape(bsz, s, -1)
    qk_d = _proj_call(h1f, w_qk_d, col0=0, ncols=2 * D_MODEL, out_dtype=BF16, epi="rope_dif",
                      tables=_dif_tables(s), seq=s, name="proj_qk_dif").reshape(bsz, s, -1)
    vgg = _proj_call(h1f, w_vgg, col0=0, ncols=3 * D_MODEL, out_dtype=BF16,
                     name="proj_v_gates").reshape(bsz, s, -1)
    kr_c = _proj_call(hcf, w_qk_r, col0=D_MODEL, ncols=D_MODEL, out_dtype=F32, scale=RET_DK ** -0.5,
                      name="proj_ctx_kr").reshape(bsz, p_len, -1)
    vr_c = _proj_call(hcf, w_v_r, col0=0, ncols=RET_HEADS * RET_DV, out_dtype=BF16,
                      name="proj_ctx_vr").reshape(bsz, p_len, -1)
    kd_c = _proj_call(hcf, w_qk_d, col0=D_MODEL, ncols=D_MODEL, out_dtype=BF16,
                      name="proj_ctx_kd").reshape(bsz, p_len, -1)
    vd_c = _proj_call(hcf, w_vgg, col0=0, ncols=D_MODEL, out_dtype=BF16,
                      name="proj_ctx_vd").reshape(bsz, p_len, -1)

    log_g = jax.nn.log_sigmoid(ret_decay_logit[0].astype(F32))
    lg_tab = jnp.zeros((RET_HEADS, 8, LANES), F32).at[:, :2, :].set(
        jnp.broadcast_to(log_g.T[:, :, None], (RET_HEADS, 2, LANES)))
    z_ret = _ret_call(lg_tab, qk_r, v_r, sg_r, kr_c, vr_c)
    z_dif = _diff_call(diff_lambda[0].astype(F32), row2(diff_subln_g[0]), qk_d, vgg, kd_c, vd_c)

    x1, h2 = _merge_call(z_ret, z_dif, vgg, x, mod, b_gate[0].reshape(2, d), w_ret_out[0].astype(BF16),
                         w_diff_out[0].astype(BF16), w_o[0].astype(BF16), row2(ln_in_g), row2(ln_in_b),
                         row2(ln1_g[0]), row2(ln1_b[0]))

    return _ffn_call(h2, x1, mod, w_up[0].astype(BF16), conv_w[0], row2(conv_b[0]), w_down[0].astype(BF16),
                     row2(ln2_g[0]), row2(ln2_b[0]))
```

```python
import functools
import math

import numpy as np
import jax
import jax.numpy as jnp
from jax import lax
from jax.experimental import pallas as pl
from jax.experimental.pallas import tpu as pltpu

F32 = jnp.float32
BF16 = jnp.bfloat16

D_MODEL = 1024
GRID_W = 64
RET_HEADS = 4
RET_DK = D_MODEL // RET_HEADS
RET_DV = 2 * RET_DK
DIFF_DH = 64
DIFF_HEADS = D_MODEL // (2 * DIFF_DH)
D_FF = ((8 * D_MODEL // 3 + 127) // 128) * 128
CONV_W = 3
ROPE_BASE = 10000.0
LN_EPS = 1e-5
DEPTH = 1
ALPHA = (2.0 * DEPTH) ** 0.25
LAM_INIT = 0.8 - 0.6 * math.exp(-0.3 * 0)

OFF_QR, OFF_KR, OFF_VR, OFF_GR = 0, RET_HEADS * RET_DK, 2 * RET_HEADS * RET_DK, 2 * RET_HEADS * RET_DK + RET_HEADS * RET_DV
OFF_QD = OFF_GR + RET_HEADS * RET_DV
OFF_KD = OFF_QD + D_MODEL
OFF_VD = OFF_KD + D_MODEL
OFF_GATE_R = OFF_VD + D_MODEL
OFF_GATE_D = OFF_GATE_R + D_MODEL
N_IN = OFF_GATE_D + D_MODEL

LANES = 128
RET_CHUNK = 256
VMEM_LIMIT = 60 * 1024 * 1024


def _cparams(sem):
    return pltpu.CompilerParams(dimension_semantics=sem, vmem_limit_bytes=VMEM_LIMIT)


def _layer_norm(x, g, b):
    mu = jnp.mean(x, axis=-1, keepdims=True)
    xc = x - mu
    var = jnp.mean(xc * xc, axis=-1, keepdims=True)
    return xc * lax.rsqrt(var + LN_EPS) * g + b


def _dot(a, b):
    return jnp.dot(a, b, preferred_element_type=F32)


def _dot_nt(a, b):
    return lax.dot_general(a, b, (((1,), (1,)), ((), ())), preferred_element_type=F32)


def _dot_tn(a, b):
    return lax.dot_general(a, b, (((0,), (0,)), ((), ())), preferred_element_type=F32)


def _mod_kernel(c_ref, w_ref, b_ref, o_ref):
    cond = jax.nn.silu(c_ref[...])
    o_ref[...] = _dot(cond.astype(BF16), w_ref[...].astype(BF16)) + b_ref[...]


def _mod_call(cc, w_mod, b_mod):
    rows, d = cc.shape
    n = w_mod.shape[1]
    tn = 1024
    return pl.pallas_call(
        _mod_kernel,
        out_shape=jax.ShapeDtypeStruct((rows, n), F32),
        grid=(n // tn,),
        in_specs=[pl.BlockSpec((rows, d), lambda j: (0, 0)),
                  pl.BlockSpec((d, tn), lambda j: (0, j)),
                  pl.BlockSpec((1, tn), lambda j: (0, j))],
        out_specs=pl.BlockSpec((rows, tn), lambda j: (0, j)),
        compiler_params=_cparams(("arbitrary",)),
        name="mod",
    )(cc, w_mod, b_mod)


def _ln_mod_kernel(x_ref, g_ref, b_ref, mod_ref, h_ref):
    xn = _layer_norm(x_ref[0], g_ref[...], b_ref[...])
    m = mod_ref[0]
    h_ref[0] = (xn * (1.0 + m[1:2]) + m[0:1]).astype(BF16)


def _ln_mod_call(x, g, b, mod, *, mod_row, ts, name):
    bsz, s, d = x.shape
    ts = min(ts, s)
    mod_map = (lambda i, j: (i, 0, 0)) if mod_row is None else (lambda i, j: (mod_row, 0, 0))
    return pl.pallas_call(
        _ln_mod_kernel,
        out_shape=jax.ShapeDtypeStruct((bsz, s, d), BF16),
        grid=(bsz, s // ts),
        in_specs=[pl.BlockSpec((1, ts, d), lambda i, j: (i, j, 0)),
                  pl.BlockSpec((1, d), lambda i, j: (0, 0)),
                  pl.BlockSpec((1, d), lambda i, j: (0, 0)),
                  pl.BlockSpec((1, 6, d), mod_map)],
        out_specs=pl.BlockSpec((1, ts, d), lambda i, j: (i, j, 0)),
        compiler_params=_cparams(("arbitrary", "arbitrary")),
        name=name,
    )(x, g, b, mod)


PROJ_ROWS = 256


def _proj_kernel(x_ref, w_ref, *rest, epi, scale, tn):
    o_ref = rest[-1]
    tm = x_ref.shape[0]
    rc = min(PROJ_ROWS, tm)
    for r in range(tm // rc):
        rs = slice(r * rc, (r + 1) * rc)
        acc = _dot(x_ref[rs, :], w_ref[...])
        if epi == "plain":
            if scale != 1.0:
                acc = acc * scale
            o_ref[rs, :] = acc.astype(o_ref.dtype)
        elif epi == "silu":
            o_ref[rs, :] = jax.nn.silu(acc).astype(o_ref.dtype)
        elif epi == "rope_ret":
            cos, sin = rest[0][0, rs, :], rest[1][0, rs, :]
            for h in range(tn // (2 * LANES)):
                a = h * 2 * LANES
                x1 = acc[:, a:a + LANES]
                x2 = acc[:, a + LANES:a + 2 * LANES]
                o_ref[rs, a:a + LANES] = (x1 * cos - x2 * sin).astype(o_ref.dtype)
                o_ref[rs, a + LANES:a + 2 * LANES] = (x2 * cos + x1 * sin).astype(o_ref.dtype)
        else:
            cos, sin = rest[0][0, rs, :], rest[1][0, rs, :]
            for g in range(tn // LANES):
                a = g * LANES
                xg = acc[:, a:a + LANES]
                o_ref[rs, a:a + LANES] = (xg * cos + pltpu.roll(xg, LANES // 2, axis=1) * sin).astype(o_ref.dtype)


def _proj_call(x, w, *, col0, ncols, out_dtype, epi="plain", scale=1.0, tables=None, seq=None, tm=2048, tn=1024, name="proj"):
    m, k = x.shape
    tm = min(tm, m if seq is None else seq)
    tn = min(tn, ncols)
    assert m % tm == 0 and ncols % tn == 0 and col0 % tn == 0
    jb = col0 // tn
    in_specs = [pl.BlockSpec((tm, k), lambda j, i: (i, 0)),
                pl.BlockSpec((k, tn), lambda j, i: (0, jb + j))]
    args = [x, w]
    if tables is not None:
        assert seq % tm == 0
        ns = seq // tm
        tspec = pl.BlockSpec((1, tm, LANES), lambda j, i: (j, i % ns, 0))
        in_specs += [tspec, tspec]
        args += list(tables)
    return pl.pallas_call(
        functools.partial(_proj_kernel, epi=epi, scale=scale, tn=tn),
        out_shape=jax.ShapeDtypeStruct((m, ncols), out_dtype),
        grid=(ncols // tn, m // tm),
        in_specs=in_specs,
        out_specs=pl.BlockSpec((tm, tn), lambda j, i: (i, j)),
        compiler_params=_cparams(("arbitrary", "arbitrary")),
        name=name,
    )(*args)


def _ret_kernel(lg_ref, q_ref, k_ref, v_ref, g_ref, kc_ref, vc_ref, o_ref, y_scr, sf_scr, sb_scr, *, chunk, n_chunks):
    c_len = chunk
    lg = lg_ref[0]
    lgf = lg[0:1, :]
    lgb = lg[1:2, :]

    def col(x):
        return x[:, 0:1]

    p_len = kc_ref.shape[1]
    pos_p = lax.broadcasted_iota(jnp.int32, (p_len, LANES), 0).astype(F32)
    kc = kc_ref[0]
    vc = vc_ref[0]
    sf_scr[...] = _dot_tn((kc * col(jnp.exp(((p_len - 1.0) - pos_p) * lgf))).astype(BF16), vc)
    sb_scr[...] = _dot_tn((kc * col(jnp.exp(pos_p * lgb))).astype(BF16), vc)

    pos = lax.broadcasted_iota(jnp.int32, (c_len, LANES), 0).astype(F32)
    dq_f = col(jnp.exp((pos + 1.0) * lgf))
    dk_f = col(jnp.exp((c_len - 1.0 - pos) * lgf))
    dq_b = col(jnp.exp((c_len - pos) * lgb))
    dk_b = col(jnp.exp(pos * lgb))
    ds_f = col(jnp.exp(c_len * lgf))
    ds_b = col(jnp.exp(c_len * lgb))
    rel = (lax.broadcasted_iota(jnp.int32, (c_len, c_len), 0)
           - lax.broadcasted_iota(jnp.int32, (c_len, c_len), 1)).astype(F32)
    d_in = (jnp.where(rel >= 0, jnp.exp(jnp.maximum(rel, 0.0) * col(lgf)), 0.0)
            + jnp.where(rel <= 0, jnp.exp(jnp.maximum(-rel, 0.0) * col(lgb)), 0.0))

    def rows(c):
        return pl.ds(pl.multiple_of(c * c_len, c_len), c_len)

    def fwd_part(c):
        q, k, v = q_ref[0, rows(c), :], k_ref[0, rows(c), :], v_ref[0, rows(c), :]
        inner = _dot_nt(q, k) * d_in
        s = sf_scr[...]
        y = _dot(inner.astype(BF16), v) + _dot(q, s.astype(BF16)) * dq_f
        sf_scr[...] = s * ds_f + _dot_tn((k.astype(F32) * dk_f).astype(BF16), v)
        return y

    def bwd_part(c):
        q, k, v = q_ref[0, rows(c), :], k_ref[0, rows(c), :], v_ref[0, rows(c), :]
        s = sb_scr[...]
        y = _dot(q, s.astype(BF16)) * dq_b
        sb_scr[...] = s * ds_b + _dot_tn((k.astype(F32) * dk_b).astype(BF16), v)
        return y

    def finish(c, y):
        mu = jnp.mean(y, axis=-1, keepdims=True)
        yc = y - mu
        var = jnp.mean(yc * yc, axis=-1, keepdims=True)
        gate = g_ref[0, rows(c), :].astype(F32)
        o_ref[0, rows(c), :] = (gate * (yc * lax.rsqrt(var + LN_EPS))).astype(o_ref.dtype)

    def first_half(i, carry):
        y_scr[rows(i), :] = fwd_part(i)
        y_scr[rows(n_chunks - 1 - i), :] = bwd_part(n_chunks - 1 - i)
        return carry

    def second_half(i, carry):
        cb = n_chunks - 1 - i
        finish(i, y_scr[rows(i), :] + fwd_part(i))
        finish(cb, y_scr[rows(cb), :] + bwd_part(cb))
        return carry

    lax.fori_loop(0, n_chunks // 2, first_half, 0, unroll=4)
    lax.fori_loop(n_chunks // 2, n_chunks, second_half, 0, unroll=2)


def _ret_call(lg_tab, qk, v, sg, krc, vrc):
    bsz, s, _ = qk.shape
    p_len = krc.shape[1]
    chunk = min(RET_CHUNK, s)
    assert s % (2 * chunk) == 0
    return pl.pallas_call(
        functools.partial(_ret_kernel, chunk=chunk, n_chunks=s // chunk),
        out_shape=jax.ShapeDtypeStruct((bsz, s, RET_HEADS * RET_DV), BF16),
        grid=(bsz, RET_HEADS),
        in_specs=[pl.BlockSpec((1, 8, LANES), lambda b, h: (h, 0, 0)),
                  pl.BlockSpec((1, s, RET_DK), lambda b, h: (b, 0, h)),
                  pl.BlockSpec((1, s, RET_DK), lambda b, h: (b, 0, RET_HEADS + h)),
                  pl.BlockSpec((1, s, RET_DV), lambda b, h: (b, 0, h)),
                  pl.BlockSpec((1, s, RET_DV), lambda b, h: (b, 0, h)),
                  pl.BlockSpec((1, p_len, RET_DK), lambda b, h: (b, 0, h)),
                  pl.BlockSpec((1, p_len, RET_DV), lambda b, h: (b, 0, h))],
        out_specs=pl.BlockSpec((1, s, RET_DV), lambda b, h: (b, 0, h)),
        scratch_shapes=[pltpu.VMEM((s, RET_DV), F32),
                        pltpu.VMEM((RET_DK, RET_DV), F32),
                        pltpu.VMEM((RET_DK, RET_DV), F32)],
        compiler_params=_cparams(("arbitrary", "arbitrary")),
        name="retention",
    )(lg_tab, qk, qk, v, sg, krc, vrc)


DIFF_TQI = 128
DIFF_TK = 256
DIFF_GROUP = 14
ONES_ROWS = 16
NEG_BIG = -1e30


def _diff_kernel(dl_ref, gsub_ref, q_ref, k_ref, v_ref, kc_ref, vc_ref, o_ref, vt_scr, sa_scr, sb_scr, q2_scr,
                 *, s_len, p_len):
    tk = min(DIFF_TK, s_len)
    blocks = [(0, r, r, tk) for r in range(0, s_len, tk)] + [(1, 0, s_len, p_len)]

    tqi = DIFF_TQI
    hd = 2 * DIFF_DH
    n_sub = s_len // tqi

    for src, r, a, n in blocks:
        v_blk = (vc_ref if src else v_ref)[0, r:r + n, :]
        vt_scr[0:hd, a:a + n] = v_blk.astype(F32).T.astype(BF16)
    vt_scr[hd:, :] = jnp.ones((ONES_ROWS, s_len + p_len), BF16)

    feat = lax.broadcasted_iota(jnp.int32, (hd, tqi), 0)
    comp0 = (feat & (hd // 4)) == 0

    def prep_q(t):
        if isinstance(t, int):
            if t >= n_sub:
                return
            row0 = t * tqi
        else:
            t = jnp.minimum(t, n_sub - 1)
            row0 = pl.multiple_of(t * tqi, tqi)
        qt = q_ref[0, pl.ds(row0, tqi), :].astype(F32).T
        q2_scr[t] = jnp.concatenate([jnp.where(comp0, qt, 0.0), jnp.where(comp0, 0.0, qt)], axis=1).astype(BF16)

    prep_q(0)

    dl = dl_ref[...]
    lam = (jnp.exp(jnp.sum(dl[0:1] * dl[1:2], axis=-1, keepdims=True))
           - jnp.exp(jnp.sum(dl[2:3] * dl[3:4], axis=-1, keepdims=True)) + LAM_INIT)
    gsub = gsub_ref[...] * (1.0 - LAM_INIT)


    def finish(t, acc):
        o2 = acc[0:hd] * (1.0 / acc[hd:hd + 1])
        o = (o2[:, :tqi] - lam * o2[:, tqi:]).T
        o = o * lax.rsqrt(jnp.mean(o * o, axis=-1, keepdims=True) + LN_EPS) * gsub
        row0 = t * tqi if isinstance(t, int) else pl.multiple_of(t * tqi, tqi)
        o_ref[0, pl.ds(row0, tqi), :] = o.astype(o_ref.dtype)

    def stage(t_val, s_val, mx, t_score, s_score, fin):
        mx_next = jnp.full((8, 2 * tqi), NEG_BIG, F32)
        acc = None
        if fin is not None:
            finish(*fin)
        if t_val is not None:
            m = jnp.max(mx, axis=0, keepdims=True)
            acc = jnp.zeros((hd + ONES_ROWS, 2 * tqi), F32)
        if t_score is not None:
            q2 = q2_scr[t_score]
        for src, r, a, n in blocks:
            if t_val is not None:
                p = jnp.exp2(s_val[a:a + n, :] - m)
                acc = acc + _dot(vt_scr[:, a:a + n], p.astype(BF16))
            if t_score is not None:
                s = _dot((kc_ref if src else k_ref)[0, r:r + n, :], q2)
                s_score[a:a + n, :] = s
                mx_next = jnp.maximum(mx_next, jnp.max(s.reshape(n // 8, 8, 2 * tqi), axis=0))
        if t_score is not None:
            prep_q(t_score + 1)
        return mx_next, acc

    def full_stage(t, odd, mx, acc):
        s_val, s_score = (sb_scr, sa_scr) if odd else (sa_scr, sb_scr)
        return stage(t, s_val, mx, t + 1, s_score, (t - 1, acc))

    mx, _ = stage(None, None, None, 0, sa_scr, None)
    mx, acc = stage(0, sa_scr, mx, 1, sb_scr, None)
    mx, acc = full_stage(1, True, mx, acc)
    mx, acc = full_stage(2, False, mx, acc)

    def stage_group(u, carry):
        mx, acc = carry
        for i in range(group):
            mx, acc = full_stage(group * u + 3 + i, i % 2 == 0, mx, acc)
        return mx, acc

    group = max(g for g in range(2, DIFF_GROUP + 1, 2) if (n_sub - 4) % g == 0)
    mx, acc = lax.fori_loop(0, (n_sub - 4) // group, stage_group, (mx, acc))
    _, acc_last = stage(n_sub - 1, sb_scr, mx, None, None, (n_sub - 2, acc))
    finish(n_sub - 1, acc_last)


def _diff_call(dl, gsub, qk, vgg, kdc, vdc):
    bsz, s, _ = qk.shape
    p_len = kdc.shape[1]
    hd = 2 * DIFF_DH
    n_keys = s + p_len
    assert s % min(DIFF_TK, s) == 0 and p_len % (2 * LANES) == 0 and s % (4 * DIFF_TQI) == 0
    return pl.pallas_call(
        functools.partial(_diff_kernel, s_len=s, p_len=p_len),
        out_shape=jax.ShapeDtypeStruct((bsz, s, D_MODEL), BF16),
        scratch_shapes=[pltpu.VMEM((hd + ONES_ROWS, n_keys), BF16),
                        pltpu.VMEM((n_keys, 2 * DIFF_TQI), F32),
                        pltpu.VMEM((n_keys, 2 * DIFF_TQI), F32),
                        pltpu.VMEM((s // DIFF_TQI, hd, 2 * DIFF_TQI), BF16)],
        grid=(bsz, DIFF_HEADS),
        in_specs=[pl.BlockSpec((4, DIFF_DH), lambda b, h: (0, 0)),
                  pl.BlockSpec((1, hd), lambda b, h: (0, 0)),
                  pl.BlockSpec((1, s, hd), lambda b, h: (b, 0, h)),
                  pl.BlockSpec((1, s, hd), lambda b, h: (b, 0, DIFF_HEADS + h)),
                  pl.BlockSpec((1, s, hd), lambda b, h: (b, 0, h)),
                  pl.BlockSpec((1, p_len, hd), lambda b, h: (b, 0, h)),
                  pl.BlockSpec((1, p_len, hd), lambda b, h: (b, 0, h))],
        out_specs=pl.BlockSpec((1, s, hd), lambda b, h: (b, 0, h)),
        compiler_params=_cparams(("arbitrary", "arbitrary")),
        name="diff_attn",
    )(dl, gsub, qk, qk, vgg, kdc, vdc)


MERGE_ROWS = 512


def _merge_kernel(zr_ref, zd_ref, gr_ref, gd_ref, x_ref, mod_ref, bg_ref, wr_ref, wd_ref, wo_ref, gi_ref, bi_ref,
                  g1_ref, b1_ref, x1_ref, h2_ref):
    bg = bg_ref[...]
    m = mod_ref[0]
    tm = x_ref.shape[1]
    rc = min(MERGE_ROWS, tm)
    for r in range(tm // rc):
        rs = slice(r * rc, (r + 1) * rc)
        y_ret = _dot(zr_ref[0, rs, :], wr_ref[...])
        y_dif = _dot(zd_ref[0, rs, :], wd_ref[...])
        gate_r = jax.nn.sigmoid(gr_ref[0, rs, :].astype(F32) + bg[0:1])
        gate_d = jax.nn.sigmoid(gd_ref[0, rs, :].astype(F32) + bg[1:2])
        y_mix = _dot((gate_r * y_ret + gate_d * y_dif).astype(BF16), wo_ref[...])
        xn = _layer_norm(x_ref[0, rs, :], gi_ref[...], bi_ref[...])
        x1 = _layer_norm(ALPHA * xn + m[2:3] * y_mix, g1_ref[...], b1_ref[...])
        x1_ref[0, rs, :] = x1
        h2_ref[0, rs, :] = (x1 * (1.0 + m[4:5]) + m[3:4]).astype(BF16)


def _merge_call(z_ret, z_dif, vgg, x, mod, bg, w_ret, w_dif, w_o, gi, bi, g1, b1, *, tm=1024):
    bsz, s, d = x.shape
    tm = min(tm, s)
    const = lambda b, i: (0, 0)
    return pl.pallas_call(
        _merge_kernel,
        out_shape=[jax.ShapeDtypeStruct((bsz, s, d), F32), jax.ShapeDtypeStruct((bsz, s, d), BF16)],
        grid=(bsz, s // tm),
        in_specs=[pl.BlockSpec((1, tm, RET_HEADS * RET_DV), lambda b, i: (b, i, 0)),
                  pl.BlockSpec((1, tm, d), lambda b, i: (b, i, 0)),
                  pl.BlockSpec((1, tm, d), lambda b, i: (b, i, 1)),
                  pl.BlockSpec((1, tm, d), lambda b, i: (b, i, 2)),
                  pl.BlockSpec((1, tm, d), lambda b, i: (b, i, 0)),
                  pl.BlockSpec((1, 6, d), lambda b, i: (b, 0, 0)),
                  pl.BlockSpec((2, d), const),
                  pl.BlockSpec(w_ret.shape, const, pipeline_mode=pl.Buffered(1)),
                  pl.BlockSpec(w_dif.shape, const, pipeline_mode=pl.Buffered(1)),
                  pl.BlockSpec(w_o.shape, const, pipeline_mode=pl.Buffered(1)),
                  pl.BlockSpec((1, d), const),
                  pl.BlockSpec((1, d), const),
                  pl.BlockSpec((1, d), const),
                  pl.BlockSpec((1, d), const)],
        out_specs=[pl.BlockSpec((1, tm, d), lambda b, i: (b, i, 0)),
                   pl.BlockSpec((1, tm, d), lambda b, i: (b, i, 0))],
        compiler_params=_cparams(("arbitrary", "arbitrary")),
        name="merge",
    )(z_ret, z_dif, vgg, vgg, x, mod, bg, w_ret, w_dif, w_o, gi, bi, g1, b1)


HALO = 16
FF_CHUNK = 256
FF_GROUP = 6


def _ffn_kernel(h_ref, hp_ref, hn_ref, x1_ref, mod_ref, wu_ref, cw_ref, cb_ref, wd_ref, g2_ref, b2_ref, o_ref,
                *, tm, n_tiles):
    i = pl.program_id(1)
    h = h_ref[0]
    h_prev = jnp.where(i > 0, hp_ref[0], jnp.zeros_like(hp_ref[0]))
    h_next = jnp.where(i < n_tiles - 1, hn_ref[0], jnp.zeros_like(hn_ref[0]))
    h_ext = jnp.concatenate([h_prev, h, h_next], axis=0)
    rows = tm + 2 * HALO
    acc = jnp.zeros((tm, D_MODEL), F32)
    n_chunks = D_FF // FF_CHUNK
    acts = []
    for j in range(n_chunks):
        sl = slice(j * FF_CHUNK, (j + 1) * FF_CHUNK)
        u = _dot(h_ext, wu_ref[:, sl])
        gate = _dot(h, wu_ref[:, D_FF + j * FF_CHUNK:D_FF + (j + 1) * FF_CHUNK])
        cw = cw_ref[:, sl]
        t = (cb_ref[:, sl] + pltpu.roll(u, 1, axis=0)[HALO:HALO + tm] * cw[0:1] + u[HALO:HALO + tm] * cw[1:2]
             + pltpu.roll(u, rows - 1, axis=0)[HALO:HALO + tm] * cw[2:3])
        act = 0.5 * t * (1.0 + lax.erf(t * np.float32(np.sqrt(0.5)))) * gate
        acts.append(act.astype(BF16))
        if len(acts) == FF_GROUP or j == n_chunks - 1:
            k0 = (j + 1 - len(acts)) * FF_CHUNK
            acc = acc + _dot(jnp.concatenate(acts, axis=1), wd_ref[k0:(j + 1) * FF_CHUNK, :])
            acts = []
    m = mod_ref[0]
    o_ref[0] = _layer_norm(ALPHA * x1_ref[0] + m[5:6] * acc, g2_ref[...], b2_ref[...])


def _ffn_call(h2, x1, mod, w_up, conv_w, conv_b, w_down, g2, b2, *, tm=1024):
    bsz, s, d = x1.shape
    tm = min(tm, s)
    n_tiles = s // tm
    hb = tm // HALO
    n_halo = s // HALO
    const = lambda b, i: (0, 0)
    return pl.pallas_call(
        functools.partial(_ffn_kernel, tm=tm, n_tiles=n_tiles),
        out_shape=jax.ShapeDtypeStruct((bsz, s, d), F32),
        grid=(bsz, n_tiles),
        in_specs=[pl.BlockSpec((1, tm, d), lambda b, i: (b, i, 0)),
                  pl.BlockSpec((1, HALO, d), lambda b, i: (b, jnp.maximum(i * hb - 1, 0), 0)),
                  pl.BlockSpec((1, HALO, d), lambda b, i: (b, jnp.minimum((i + 1) * hb, n_halo - 1), 0)),
                  pl.BlockSpec((1, tm, d), lambda b, i: (b, i, 0)),
                  pl.BlockSpec((1, 6, d), lambda b, i: (b, 0, 0)),
                  pl.BlockSpec((d, 2 * D_FF), const, pipeline_mode=pl.Buffered(1)),
                  pl.BlockSpec((CONV_W, D_FF), const),
                  pl.BlockSpec((1, D_FF), const),
                  pl.BlockSpec((D_FF, d), const, pipeline_mode=pl.Buffered(1)),
                  pl.BlockSpec((1, d), const),
                  pl.BlockSpec((1, d), const)],
        out_specs=pl.BlockSpec((1, tm, d), lambda b, i: (b, i, 0)),
        compiler_params=_cparams(("arbitrary", "arbitrary")),
        name="ffn",
    )(h2, h2, h2, x1, mod, w_up, conv_w, conv_b, w_down, g2, b2)


def _axial_angles(s, head_dim):
    rows = s // GRID_W
    row = np.repeat(np.arange(rows, dtype=np.float64), GRID_W)
    col = np.tile(np.arange(GRID_W, dtype=np.float64), rows)
    half = head_dim // 2
    inv = ROPE_BASE ** (-(np.arange(0, half, 2, dtype=np.float64) / half))
    return np.concatenate([row[:, None] * inv, col[:, None] * inv], axis=-1)


def _as_tables(cos_pair, sin_pair):
    return jnp.asarray(np.stack(cos_pair), F32), jnp.asarray(np.stack(sin_pair), F32)


def _ret_tables(s):
    ang = _axial_angles(s, RET_DK)
    cos, sin = np.cos(ang), np.sin(ang)
    k_scale = RET_DK ** -0.5
    return _as_tables([cos, cos * k_scale], [sin, sin * k_scale])


def _dif_tables(s):
    ang = _axial_angles(s, DIFF_DH)
    cos, sin = np.cos(ang), np.sin(ang)
    cos_t = np.tile(cos, (1, 4))
    sin_t = np.concatenate([-sin, -sin, sin, sin], axis=-1)
    q_scale = DIFF_DH ** -0.5 * math.log2(math.e)
    return _as_tables([cos_t * q_scale, cos_t], [sin_t * q_scale, sin_t])


def _dif_col_perm():
    q4 = DIFF_DH // 2
    g = np.concatenate([np.arange(0, q4), np.arange(2 * q4, 3 * q4), np.arange(q4, 2 * q4), np.arange(3 * q4, 4 * q4)])
    return np.concatenate([h * LANES + g for h in range(2 * DIFF_HEADS)])


def kernel(x, c, ctx, c_ctx, ln_in_g, ln_in_b, w_mod, b_mod, w_in, b_gate, ret_decay_logit, diff_lambda, diff_subln_g,
           w_ret_out, w_diff_out, w_o, ln1_g, ln1_b, w_up, conv_w, conv_b, w_down, ln2_g, ln2_b):
    bsz, s, d = x.shape
    p_len = ctx.shape[1]
    row2 = lambda v: v.reshape(1, -1)

    n_rows = ((bsz + 1 + 7) // 8) * 8
    cc = jnp.zeros((n_rows, d), F32).at[:bsz].set(c).at[bsz].set(c_ctx)
    mod = _mod_call(cc, w_mod[0], row2(b_mod[0])).reshape(n_rows, 6, d)

    h1 = _ln_mod_call(x, row2(ln_in_g), row2(ln_in_b), mod, mod_row=None, ts=1024, name="ln_mod_x")
    hc = _ln_mod_call(ctx, row2(ln_in_g), row2(ln_in_b), mod, mod_row=bsz, ts=256, name="ln_mod_ctx")

    w = w_in[0]
    w_qk_r = w[:, OFF_QR:OFF_VR].astype(BF16)
    w_v_r = w[:, OFF_VR:OFF_GR].astype(BF16)
    w_g_r = w[:, OFF_GR:OFF_QD].astype(BF16)
    w_qk_d = w[:, OFF_QD:OFF_VD][:, _dif_col_perm()].astype(BF16)
    w_vgg = w[:, OFF_VD:].astype(BF16)

    h1f = h1.reshape(bsz * s, d)
    hcf = hc.reshape(bsz * p_len, d)
    qk_r = _proj_call(h1f, w_qk_r, col0=0, ncols=2 * D_MODEL, out_dtype=BF16, epi="rope_ret",
                      tables=_ret_tables(s), seq=s, name="proj_qk_ret").reshape(bsz, s, -1)
    v_r = _proj_call(h1f, w_v_r, col0=0, ncols=RET_HEADS * RET_DV, out_dtype=BF16,
                     name="proj_v_ret").reshape(bsz, s, -1)
    sg_r = _proj_call(h1f, w_g_r, col0=0, ncols=RET_HEADS * RET_DV, out_dtype=BF16, epi="silu",
                      name="proj_g_ret").reshape(bsz, s, -1)
    qk_d = _proj_call(h1f, w_qk_d, col0=0, ncols=2 * D_MODEL, out_dtype=BF16, epi="rope_dif",
                      tables=_dif_tables(s), seq=s, name="proj_qk_dif").reshape(bsz, s, -1)
    vgg = _proj_call(h1f, w_vgg, col0=0, ncols=3 * D_MODEL, out_dtype=BF16,
                     name="proj_v_gates").reshape(bsz, s, -1)
    kr_c = _proj_call(hcf, w_qk_r, col0=D_MODEL, ncols=D_MODEL, out_dtype=F32, scale=RET_DK ** -0.5,
                      name="proj_ctx_kr").reshape(bsz, p_len, -1)
    vr_c = _proj_call(hcf, w_v_r, col0=0, ncols=RET_HEADS * RET_DV, out_dtype=BF16,
                      name="proj_ctx_vr").reshape(bsz, p_len, -1)
    kd_c = _proj_call(hcf, w_qk_d, col0=D_MODEL, ncols=D_MODEL, out_dtype=BF16,
                      name="proj_ctx_kd").reshape(bsz, p_len, -1)
    vd_c = _proj_call(hcf, w_vgg, col0=0, ncols=D_MODEL, out_dtype=BF16,
                      name="proj_ctx_vd").reshape(bsz, p_len, -1)

    log_g = jax.nn.log_sigmoid(ret_decay_logit[0].astype(F32))
    lg_tab = jnp.zeros((RET_HEADS, 8, LANES), F32).at[:, :2, :].set(
        jnp.broadcast_to(log_g.T[:, :, None], (RET_HEADS, 2, LANES)))
    z_ret = _ret_call(lg_tab, qk_r, v_r, sg_r, kr_c, vr_c)
    z_dif = _diff_call(diff_lambda[0].astype(F32), row2(diff_subln_g[0]), qk_d, vgg, kd_c, vd_c)

    x1, h2 = _merge_call(z_ret, z_dif, vgg, x, mod, b_gate[0].reshape(2, d), w_ret_out[0].astype(BF16),
                         w_diff_out[0].astype(BF16), w_o[0].astype(BF16), row2(ln_in_g), row2(ln_in_b),
                         row2(ln1_g[0]), row2(ln1_b[0]))

    return _ffn_call(h2, x1, mod, w_up[0].astype(BF16), conv_w[0], row2(conv_b[0]), w_down[0].astype(BF16),
                     row2(ln2_g[0]), row2(ln2_b[0]))
```
